```python
import math
import jax, jax.numpy as jnp
from jax import lax
import numpy as np

D_MODEL = 4096
BATCH = 4
SEQ = 4096
DEPTH = 4

GRID_W = 64
CTX_LEN = 256
N_MIXERS = 2
HEAD_DIM = 128
V_HEAD_DIM = 2 * HEAD_DIM
N_HEADS = D_MODEL // (2 * HEAD_DIM)
ROPE_HALF = HEAD_DIM // 2
ROPE_BASE = 10000.0
Q_BLOCK = 128
POOL_WINDOWS = (2, 4, 8, 16)
N_POOL_GROUPS = len(POOL_WINDOWS)
POOL_GROUP = D_MODEL // N_POOL_GROUPS
D_FF = 7168
CONV_W = 3
N_ATTN_LAYERS = (DEPTH + N_MIXERS - 1) // N_MIXERS
N_POOL_LAYERS = DEPTH // N_MIXERS
EPS = 1e-6
SUBLN_EPS = 1e-5

kernel_name = 'hybrid_diffattn_pool_convffn_dit'


def _rms_norm(x, g, eps=EPS):
    xf = x.astype(jnp.float32)
    y = xf * lax.rsqrt(jnp.mean(xf * xf, axis=-1, keepdims=True) + eps)
    return (y * g.astype(jnp.float32)).astype(x.dtype)


def _modulate(h, shift, scale):
    return h * (1.0 + scale) + shift


def _rotate_half(x):
    x1, x2 = jnp.split(x, 2, axis=-1)
    return jnp.concatenate([-x2, x1], axis=-1)


def _axial_rope_tables(n_tokens, dtype):
    rows = n_tokens // GRID_W
    row = jnp.repeat(jnp.arange(rows), GRID_W).astype(jnp.float32)
    col = jnp.tile(jnp.arange(GRID_W), rows).astype(jnp.float32)
    inv = 1.0 / (ROPE_BASE ** (jnp.arange(0, ROPE_HALF, 2, dtype=jnp.float32) / ROPE_HALF))

    def tab(pos):
        a = pos[:, None] * inv[None, :]
        a = jnp.concatenate([a, a], axis=-1)
        return jnp.cos(a), jnp.sin(a)

    cr, sr = tab(row)
    cc, sc = tab(col)
    cos = jnp.concatenate([cr, cc], axis=-1)
    sin = jnp.concatenate([sr, sc], axis=-1)
    return cos.astype(dtype), sin.astype(dtype)


def _rope(x, cos, sin):
    cos = cos[None, :, None, None, :]
    sin = sin[None, :, None, None, :]
    rot = jnp.concatenate([_rotate_half(x[..., :ROPE_HALF]), _rotate_half(x[..., ROPE_HALF:])], axis=-1)
    return x * cos + rot * sin


def _diff_attend(q, k, v, lam):
    s = jnp.einsum('bqhmd,bkhmd->bhmqk', q, k).astype(jnp.float32)
    p = jax.nn.softmax(s, axis=-1)
    p = p[:, :, 0] - lam * p[:, :, 1]
    return jnp.einsum('bhqk,bkhe->bqhe', p.astype(v.dtype), v)


def _diff_head_out(o, subln_g, lam_init, w_o):
    b, n = o.shape[0], o.shape[1]
    o = _rms_norm(o, subln_g, SUBLN_EPS) * (1.0 - lam_init)
    return o.reshape(b, n, D_MODEL) @ w_o


def _diff_attention(h_lat, h_ctx, w_qkv, w_o, lam_p, subln_g, lam_init, need_ctx):
    b, n, _ = h_lat.shape
    nc = h_ctx.shape[1]
    scale = HEAD_DIM ** -0.5
    q, k, v = jnp.split(h_lat @ w_qkv, 3, axis=-1)
    cos, sin = _axial_rope_tables(n, h_lat.dtype)
    q = _rope(q.reshape(b, n, N_HEADS, 2, HEAD_DIM), cos, sin) * scale
    k = _rope(k.reshape(b, n, N_HEADS, 2, HEAD_DIM), cos, sin)
    v = v.reshape(b, n, N_HEADS, V_HEAD_DIM)
    if need_ctx:
        q_c, k_c, v_c = jnp.split(h_ctx @ w_qkv, 3, axis=-1)
    else:
        k_c, v_c = jnp.split(h_ctx @ w_qkv[:, D_MODEL:], 2, axis=-1)
    k_c = k_c.reshape(b, nc, N_HEADS, 2, HEAD_DIM)
    v_c = v_c.reshape(b, nc, N_HEADS, V_HEAD_DIM)
    lp = lam_p.astype(jnp.float32)
    lam = jnp.exp(jnp.sum(lp[0] * lp[1])) - jnp.exp(jnp.sum(lp[2] * lp[3])) + lam_init
    k_all = jnp.concatenate([k_c, k], axis=1)
    v_all = jnp.concatenate([v_c, v], axis=1)
    nblk = n // Q_BLOCK
    qb = q.reshape(b, nblk, Q_BLOCK, N_HEADS, 2, HEAD_DIM).swapaxes(0, 1)
    ob = lax.map(lambda blk: _diff_attend(blk, k_all, v_all, lam), qb)
    o = ob.swapaxes(0, 1).reshape(b, n, N_HEADS, V_HEAD_DIM)
    out_lat = _diff_head_out(o, subln_g, lam_init, w_o)
    out_ctx = None
    if need_ctx:
        q_c = q_c.reshape(b, nc, N_HEADS, 2, HEAD_DIM) * scale
        o_c = _diff_attend(q_c, k_c, v_c, lam)
        out_ctx = _diff_head_out(o_c, subln_g, lam_init, w_o)
    return out_lat, out_ctx


def _multiscale_pool(h, pool_w, pool_scale):
    b, n, d = h.shape
    hf = h.astype(jnp.float32)
    cs = jnp.concatenate([jnp.zeros((b, 1, d), jnp.float32), jnp.cumsum(hf, axis=1)], axis=1)
    t = jnp.arange(n)
    outs = []
    for g, w in enumerate(POOL_WINDOWS):
        sl = slice(g * POOL_GROUP, (g + 1) * POOL_GROUP)
        lo = jnp.clip(t - w // 2, 0, n - 1)
        hi = jnp.clip(t + (w - 1 - w // 2), 0, n - 1)
        csg = cs[..., sl]
        cnt = (hi - lo + 1).astype(jnp.float32)[None, :, None]
        mean = (jnp.take(csg, hi + 1, axis=1) - jnp.take(csg, lo, axis=1)) / cnt
        outs.append(mean - hf[..., sl])
    y = jnp.stack(outs, axis=2).astype(h.dtype)
    y = jnp.einsum('blgc,gce->blge', y, pool_w).reshape(b, n, d)
    return y * pool_scale


def _conv_ffn(h, w_in, conv_w, conv_b, w_out):
    u = h @ w_in
    z = jnp.zeros_like(u[:, :1])
    u_prev = jnp.concatenate([z, u[:, :-1]], axis=1)
    u_next = jnp.concatenate([u[:, 1:], z], axis=1)
    u = u_prev * conv_w[0] + u * conv_w[1] + u_next * conv_w[2] + conv_b
    gate, val = jnp.split(u, 2, axis=-1)
    return (jax.nn.silu(gate) * val) @ w_out


def _ctx_needed_after(i):
    return any((j % N_MIXERS) == 0 for j in range(i + 1, DEPTH))


def setup_inputs(seed: int = 0) -> dict:
    key = jax.random.key(seed)
    ks = jax.random.split(key, 18)
    f32 = jnp.float32
    D = D_MODEL

    def nrm(k, shape, s):
        return jax.random.normal(k, shape, f32) * s

    return {
        'x': nrm(ks[0], (BATCH, SEQ, D), 1.0),
        'c': nrm(ks[1], (BATCH, D), 1.0),
        'ctx': nrm(ks[2], (BATCH, CTX_LEN, D), 1.0),
        'c_ctx': nrm(ks[3], (D,), 1.0),
        'ada_w': nrm(ks[4], (DEPTH, D, 6 * D), 0.5 * D ** -0.5),
        'ada_b': nrm(ks[5], (DEPTH, 6 * D), 0.02),
        'norm_g': 1.0 + nrm(ks[6], (DEPTH, 2, D), 0.02),
        'attn_qkv': nrm(ks[7], (N_ATTN_LAYERS, D, 3 * D), D ** -0.5),
        'attn_o': nrm(ks[8], (N_ATTN_LAYERS, D, D), D ** -0.5),
        'attn_lambda': nrm(ks[9], (N_ATTN_LAYERS, 4, HEAD_DIM), 0.1),
        'attn_subln_g': 1.0 + nrm(ks[10], (N_ATTN_LAYERS, V_HEAD_DIM), 0.02),
        'pool_w': nrm(ks[11], (N_POOL_LAYERS, N_POOL_GROUPS, POOL_GROUP, POOL_GROUP), POOL_GROUP ** -0.5),
        'pool_scale': 1.0 + nrm(ks[12], (N_POOL_LAYERS, D), 0.1),
        'ffn_w_in': nrm(ks[13], (DEPTH, D, 2 * D_FF), D ** -0.5),
        'ffn_conv_w': nrm(ks[14], (DEPTH, CONV_W, 2 * D_FF), CONV_W ** -0.5),
        'ffn_conv_b': nrm(ks[15], (DEPTH, 2 * D_FF), 0.02),
        'ffn_w_out': nrm(ks[16], (DEPTH, D_FF, D), D_FF ** -0.5),
        'final_g': 1.0 + nrm(ks[17], (D,), 0.02),
    }


def reference(x, c, ctx, c_ctx, ada_w, ada_b, norm_g, attn_qkv, attn_o, attn_lambda, attn_subln_g,
              pool_w, pool_scale, ffn_w_in, ffn_conv_w, ffn_conv_b, ffn_w_out, final_g):
    b = x.shape[0]
    cond = jax.nn.silu(jnp.concatenate([c, c_ctx[None, :]], axis=0))
    for i in range(DEPTH):
        need_ctx = _ctx_needed_after(i)
        mod = (cond @ ada_w[i] + ada_b[i])[:, None, :]
        chunks = jnp.split(mod, 6, axis=-1)
        ml = [m[:b] for m in chunks]
        mc = [m[b:] for m in chunks]
        h = _modulate(_rms_norm(x, norm_g[i, 0]), ml[0], ml[1])
        out_c = None
        if i % N_MIXERS == 0:
            a = i // N_MIXERS
            hc = _modulate(_rms_norm(ctx, norm_g[i, 0]), mc[0], mc[1])
            lam_init = 0.8 - 0.6 * math.exp(-0.3 * i)
            out, out_c = _diff_attention(h, hc, attn_qkv[a], attn_o[a], attn_lambda[a], attn_subln_g[a],
                                         lam_init, need_ctx)
        else:
            p = i // N_MIXERS
            out = _multiscale_pool(h, pool_w[p], pool_scale[p])
            if need_ctx:
                hc = _modulate(_rms_norm(ctx, norm_g[i, 0]), mc[0], mc[1])
                out_c = _multiscale_pool(hc, pool_w[p], pool_scale[p])
        x = x + ml[2] * out
        if need_ctx:
            ctx = ctx + mc[2] * out_c
        h = _modulate(_rms_norm(x, norm_g[i, 1]), ml[3], ml[4])
        x = x + ml[5] * _conv_ffn(h, ffn_w_in[i], ffn_conv_w[i], ffn_conv_b[i], ffn_w_out[i])
        if need_ctx:
            hc = _modulate(_rms_norm(ctx, norm_g[i, 1]), mc[3], mc[4])
            ctx = ctx + mc[5] * _conv_ffn(hc, ffn_w_in[i], ffn_conv_w[i], ffn_conv_b[i], ffn_w_out[i])
    return _rms_norm(x, final_g)
```

```python
import functools
import math

import jax
import jax.numpy as jnp
from jax import lax
from jax.experimental import pallas as pl
from jax.experimental.pallas import tpu as pltpu

D_MODEL = 4096
BATCH = 4
SEQ = 4096
DEPTH = 4
GRID_W = 64
CTX_LEN = 256
N_MIXERS = 2
HEAD_DIM = 128
V_HEAD_DIM = 2 * HEAD_DIM
N_HEADS = D_MODEL // V_HEAD_DIM
ROPE_HALF = HEAD_DIM // 2
ROPE_BASE = 10000.0
POOL_WINDOWS = (2, 4, 8, 16)
N_POOL_GROUPS = len(POOL_WINDOWS)
POOL_GROUP = D_MODEL // N_POOL_GROUPS
D_FF = 7168
EPS = 1e-6
SUBLN_EPS = 1e-5

N_LAT = BATCH * SEQ
N_CTX = BATCH * CTX_LEN
N_ALL = N_LAT + N_CTX
COND_ROWS = 8

V7X_VMEM_BYTES = 64 * 1024 * 1024
VMEM_CAP_BYTES = V7X_VMEM_BYTES - 6 * 1024 * 1024
LANES = 128
BF16_SUBLANES = 16

TM = 1024
BF16 = jnp.bfloat16
F32 = jnp.float32


def _params(semantics, vmem_bytes):
    return pltpu.CompilerParams(
        dimension_semantics=semantics,
        vmem_limit_bytes=int(min(VMEM_CAP_BYTES, vmem_bytes)),
    )


def _mod_row(layer, chunk, tile_rows):
    n_lat_tiles = N_LAT // tile_rows
    tiles_per_seq = SEQ // tile_rows
    base = (layer * 6 + chunk) * COND_ROWS

    def idx(i):
        return base + jnp.where(i < n_lat_tiles, i // tiles_per_seq, BATCH)

    return idx


def _ada_kernel(cond_ref, w_ref, b_ref, o_ref):
    c = cond_ref[...]
    c = c * (1.0 / (1.0 + jnp.exp(-c)))
    acc = jnp.dot(c.astype(BF16), w_ref[...].astype(BF16), preferred_element_type=F32)
    o_ref[...] = acc + b_ref[...]


def _ada(cond, ada_w, ada_b):
    tn = 512
    n = 6 * D_MODEL
    return pl.pallas_call(
        _ada_kernel,
        grid=(DEPTH, n // tn),
        in_specs=[
            pl.BlockSpec((COND_ROWS, D_MODEL), lambda i, j: (0, 0)),
            pl.BlockSpec((None, D_MODEL, tn), lambda i, j: (i, 0, j)),
            pl.BlockSpec((None, 1, tn), lambda i, j: (i, 0, j)),
        ],
        out_specs=pl.BlockSpec((None, COND_ROWS, tn), lambda i, j: (i, 0, j)),
        out_shape=jax.ShapeDtypeStruct((DEPTH, COND_ROWS, n), F32),
        compiler_params=_params(("arbitrary", "arbitrary"), 3 * D_MODEL * tn * 4 + (8 << 20)),
        name="ada_mod",
    )(cond, ada_w, ada_b.reshape(DEPTH, 1, n))


def _norm_mod(x, g, sh, sc):
    ms = jnp.mean(x * x, axis=-1, keepdims=True)
    y = x * lax.rsqrt(ms + EPS) * g
    return y * (1.0 + sc) + sh


def _norm_mod_kernel(x_ref, g_ref, sh_ref, sc_ref, o_ref):
    o_ref[...] = _norm_mod(x_ref[...], g_ref[...], sh_ref[...], sc_ref[...]).astype(o_ref.dtype)


def _norm_mod_call(xs, g, modr, layer, chunk0, n_rows):
    r = 512
    sh = _mod_row(layer, chunk0, r)
    sc = _mod_row(layer, chunk0 + 1, r)
    return pl.pallas_call(
        _norm_mod_kernel,
        grid=(n_rows // r,),
        in_specs=[
            pl.BlockSpec((r, D_MODEL), lambda i: (i, 0)),
            pl.BlockSpec((1, D_MODEL), lambda i: (0, 0)),
            pl.BlockSpec((None, 1, D_MODEL), lambda i: (sh(i), 0, 0)),
            pl.BlockSpec((None, 1, D_MODEL), lambda i: (sc(i), 0, 0)),
        ],
        out_specs=pl.BlockSpec((r, D_MODEL), lambda i: (i, 0)),
        out_shape=jax.ShapeDtypeStruct((n_rows, D_MODEL), BF16),
        compiler_params=_params(("arbitrary",), 6 * r * D_MODEL * 4),
        name="norm_mod",
    )(xs, g.reshape(1, D_MODEL), modr, modr)


def _final_norm_kernel(x_ref, g_ref, o_ref):
    x = x_ref[...]
    ms = jnp.mean(x * x, axis=-1, keepdims=True)
    o_ref[...] = x * lax.rsqrt(ms + EPS) * g_ref[...]


def _final_norm(xs, g):
    r = 512
    return pl.pallas_call(
        _final_norm_kernel,
        grid=(N_LAT // r,),
        in_specs=[
            pl.BlockSpec((r, D_MODEL), lambda i: (i, 0)),
            pl.BlockSpec((1, D_MODEL), lambda i: (0, 0)),
        ],
        out_specs=pl.BlockSpec((r, D_MODEL), lambda i: (i, 0)),
        out_shape=jax.ShapeDtypeStruct((N_LAT, D_MODEL), F32),
        compiler_params=_params(("arbitrary",), 6 * r * D_MODEL * 4),
        name="final_norm",
    )(xs, g.reshape(1, D_MODEL))


def _qkv_kernel(a_ref, w_ref, cos_ref, sin_ref, o_ref, *, chunk, n_rope_blocks, n_q_blocks, q_scale):
    j = pl.program_id(1)
    tm, tn = o_ref.shape
    use_rope = j < n_rope_blocks
    qs = jnp.where(j < n_q_blocks, q_scale, 1.0).astype(F32)
    lane = lax.broadcasted_iota(jnp.int32, (chunk, HEAD_DIM), 1)
    first_half = (lane % ROPE_HALF) < (ROPE_HALF // 2)
    for c in range(tm // chunk):
        rows = pl.ds(c * chunk, chunk)
        acc = jnp.dot(a_ref[rows, :], w_ref[...], preferred_element_type=F32)
        cos = jnp.where(use_rope, cos_ref[rows, :], 1.0)
        sin = jnp.where(use_rope, sin_ref[rows, :], 0.0)
        for t in range(tn // HEAD_DIM):
            cols = slice(t * HEAD_DIM, (t + 1) * HEAD_DIM)
            x = acc[:, cols]
            rot = jnp.where(first_half, pltpu.roll(x, HEAD_DIM - ROPE_HALF // 2, 1),
                            pltpu.roll(x, ROPE_HALF // 2, 1))
            o_ref[rows, cols] = ((x * cos + rot * sin) * qs).astype(o_ref.dtype)


def _qkv_call(h, w, cos_tab, sin_tab, n_rows):
    tn, chunk = 512, 512
    n = 3 * D_MODEL
    kern = functools.partial(
        _qkv_kernel, chunk=chunk, n_rope_blocks=2 * D_MODEL // tn, n_q_blocks=D_MODEL // tn,
        q_scale=HEAD_DIM ** -0.5)
    vmem = 2 * (TM * D_MODEL * 2 + D_MODEL * tn * 2 + TM * tn * 2 + 2 * TM * HEAD_DIM * 4) + (12 << 20)
    return pl.pallas_call(
        kern,
        grid=(n_rows // TM, n // tn),
        in_specs=[
            pl.BlockSpec((TM, D_MODEL), lambda i, j: (i, 0)),
            pl.BlockSpec((D_MODEL, tn), lambda i, j: (0, j)),
            pl.BlockSpec((TM, HEAD_DIM), lambda i, j: (i, 0)),
            pl.BlockSpec((TM, HEAD_DIM), lambda i, j: (i, 0)),
        ],
        out_specs=pl.BlockSpec((TM, tn), lambda i, j: (i, j)),
        out_shape=jax.ShapeDtypeStruct((n_rows, n), BF16),
        compiler_params=_params(("arbitrary", "arbitrary"), vmem),
        name="qkv_rope",
    )(h, w, cos_tab, sin_tab)


def _attn_kernel(lam_ref, g_ref, q_ref, *refs, tq, tk, n_ctx_keys, n_lat_keys, lam_init):
    if n_lat_keys:
        kc_ref, vc_ref, kl_ref, vl_ref, o_ref, m_ref, l_ref, acc_ref = refs
    else:
        kc_ref, vc_ref, o_ref, m_ref, l_ref, acc_ref = refs
    m_ref[...] = jnp.full(m_ref.shape, -jnp.inf, F32)
    l_ref[...] = jnp.zeros(l_ref.shape, F32)
    acc_ref[...] = jnp.zeros(acc_ref.shape, F32)

    def step(k, v):
        n = k.shape[0]
        for m in range(2):
            cols = slice(m * HEAD_DIM, (m + 1) * HEAD_DIM)
            s = lax.dot_general(q_ref[:, cols], k[:, cols], (((1,), (1,)), ((), ())),
                                preferred_element_type=F32)
            m_prev = m_ref[m]
            m_next = jnp.maximum(m_prev, jnp.max(s, axis=1, keepdims=True))
            alpha = jnp.exp(m_prev - m_next)
            p = jnp.exp(s - jnp.concatenate([m_next] * (n // LANES), axis=1))
            l_ref[m] = alpha * l_ref[m] + jnp.sum(p, axis=1, keepdims=True)
            m_ref[m] = m_next
            pv = jnp.dot(p.astype(BF16), v, preferred_element_type=F32)
            acc_ref[m] = acc_ref[m] * jnp.concatenate([alpha] * (V_HEAD_DIM // LANES), axis=1) + pv

    step(kc_ref[...], vc_ref[...])
    if n_lat_keys:
        def body(c, carry):
            rows = pl.ds(pl.multiple_of(c * tk, tk), tk)
            step(kl_ref[rows, :], vl_ref[rows, :])
            return carry
        lax.fori_loop(0, n_lat_keys // tk, body, 0)

    lp = lam_ref[...]
    lam = (jnp.exp(jnp.sum(lp[0:1] * lp[1:2], axis=1, keepdims=True))
           - jnp.exp(jnp.sum(lp[2:3] * lp[3:4], axis=1, keepdims=True)) + lam_init)
    rep = V_HEAD_DIM // LANES
    o1 = acc_ref[0] / jnp.concatenate([l_ref[0]] * rep, axis=1)
    o2 = acc_ref[1] / jnp.concatenate([l_ref[1]] * rep, axis=1)
    o = o1 - lam * o2
    ms = jnp.mean(o * o, axis=-1, keepdims=True)
    y = o * lax.rsqrt(ms + SUBLN_EPS) * g_ref[...]
    o_ref[...] = (y * (1.0 - lam_init)).astype(o_ref.dtype)


def _attn_call(qkv, lam_p, subln_g, lam_init, latent):
    hq = D_MODEL // V_HEAD_DIM
    ctx_row0 = N_LAT // CTX_LEN
    if latent:
        tq, tk, n_q, n_lat_keys, n_out = 512, 512, SEQ // 512, SEQ, N_LAT
        q_map = lambda b, h, qi: (b * (SEQ // tq) + qi, h)
    else:
        tq, tk, n_q, n_lat_keys, n_out = CTX_LEN, 512, 1, 0, N_CTX
        q_map = lambda b, h, qi: (ctx_row0 + b, h)
    in_specs = [
        pl.BlockSpec((4, HEAD_DIM), lambda b, h, qi: (0, 0)),
        pl.BlockSpec((1, V_HEAD_DIM), lambda b, h, qi: (0, 0)),
        pl.BlockSpec((tq, V_HEAD_DIM), q_map),
        pl.BlockSpec((CTX_LEN, V_HEAD_DIM), lambda b, h, qi: (ctx_row0 + b, hq + h)),
        pl.BlockSpec((CTX_LEN, V_HEAD_DIM), lambda b, h, qi: (ctx_row0 + b, 2 * hq + h)),
    ]
    args = [lam_p, subln_g.reshape(1, V_HEAD_DIM), qkv, qkv, qkv]
    if latent:
        in_specs += [
            pl.BlockSpec((SEQ, V_HEAD_DIM), lambda b, h, qi: (b, hq + h)),
            pl.BlockSpec((SEQ, V_HEAD_DIM), lambda b, h, qi: (b, 2 * hq + h)),
        ]
        args += [qkv, qkv]
        out_map = lambda b, h, qi: (b * (SEQ // tq) + qi, h)
    else:
        out_map = lambda b, h, qi: (b, h)
    kern = functools.partial(_attn_kernel, tq=tq, tk=tk, n_ctx_keys=CTX_LEN, n_lat_keys=n_lat_keys,
                             lam_init=lam_init)
    return pl.pallas_call(
        kern,
        grid=(BATCH, N_HEADS, n_q),
        in_specs=in_specs,
        out_specs=pl.BlockSpec((tq, V_HEAD_DIM), out_map),
        out_shape=jax.ShapeDtypeStruct((n_out, D_MODEL), BF16),
        scratch_shapes=[
            pltpu.VMEM((2, tq, LANES), F32),
            pltpu.VMEM((2, tq, LANES), F32),
            pltpu.VMEM((2, tq, V_HEAD_DIM), F32),
        ],
        compiler_params=_params(("arbitrary", "arbitrary", "arbitrary"), 40 << 20),
        name="diff_attn_lat" if latent else "diff_attn_ctx",
    )(*args)


def _mm_res_kernel(a_ref, w_ref, x_ref, gate_ref, cs_ref, o_ref, *, chunk):
    tm = o_ref.shape[0]
    for c in range(tm // chunk):
        rows = pl.ds(c * chunk, chunk)
        acc = jnp.dot(a_ref[rows, :], w_ref[...], preferred_element_type=F32)
        o_ref[rows, :] = x_ref[rows, :] + gate_ref[...] * (acc * cs_ref[...])


def _mm_res_call(a, w, xs, modr, col_scale, layer, gate_chunk, row0, *, tm, tn, grouped=False):
    k = w.shape[-2]
    assert a.shape[0] % tm == 0 and row0 % tm == 0
    n_row_tiles = a.shape[0] // tm
    row_tile0 = row0 // tm
    gate = _mod_row(layer, gate_chunk, tm)
    if grouped:
        assert tn == w.shape[-1]
        a_spec = pl.BlockSpec((tm, k), lambda i, j: (i, j))
        w_spec = pl.BlockSpec((None, k, tn), lambda i, j: (j, 0, 0))
    else:
        a_spec = pl.BlockSpec((tm, k), lambda i, j: (i, 0))
        w_spec = pl.BlockSpec((k, tn), lambda i, j: (0, j))
    x_spec = pl.BlockSpec((tm, tn), lambda i, j: (row_tile0 + i, j))
    vmem = 2 * (tm * k * 2 + k * tn * 2 + 2 * tm * tn * 4) + (12 << 20)
    return pl.pallas_call(
        functools.partial(_mm_res_kernel, chunk=min(tm, 512)),
        grid=(n_row_tiles, D_MODEL // tn),
        in_specs=[
            a_spec,
            w_spec,
            x_spec,
            pl.BlockSpec((None, 1, tn), lambda i, j: (gate(row_tile0 + i), 0, j)),
            pl.BlockSpec((1, tn), lambda i, j: (0, j)),
        ],
        out_specs=x_spec,
        out_shape=jax.ShapeDtypeStruct(xs.shape, F32),
        input_output_aliases={2: 0},
        compiler_params=_params(("arbitrary", "arbitrary"), vmem),
        name="mm_residual",
    )(a, w, xs, modr, col_scale)


def _pool_kernel(x_ref, xp_ref, xn_ref, g_ref, sh_ref, sc_ref, o_ref, hs_ref, *, r, halo, n_lat_tiles):
    i = pl.program_id(0)
    seq_len = jnp.where(i < n_lat_tiles, SEQ, CTX_LEN)
    pos0 = (i * r) & (seq_len - 1)
    is_first = pos0 == 0
    is_last = pos0 + r == seq_len
    g, sh, sc = g_ref[...], sh_ref[...], sc_ref[...]
    hs_ref[0:halo, :] = jnp.where(is_first, 0.0, _norm_mod(xp_ref[...], g, sh, sc))
    hs_ref[halo:halo + r, :] = _norm_mod(x_ref[...], g, sh, sc)
    hs_ref[halo + r:, :] = jnp.where(is_last, 0.0, _norm_mod(xn_ref[...], g, sh, sc))

    rc, cc = 128, 256
    for grp, w in enumerate(POOL_WINDOWS):
        before, after = w // 2, w - 1 - w // 2
        for ri in range(r // rc):
            pos = pos0 + ri * rc + lax.broadcasted_iota(jnp.int32, (rc, 1), 0)
            lo = jnp.maximum(pos - before, 0)
            hi = jnp.minimum(pos + after, seq_len - 1)
            inv_cnt = 1.0 / (hi - lo + 1).astype(F32)
            for ci in range(POOL_GROUP // cc):
                cols = slice(grp * POOL_GROUP + ci * cc, grp * POOL_GROUP + (ci + 1) * cc)
                acc = hs_ref[pl.ds(halo + ri * rc - before, rc), cols]
                for s in range(-before + 1, after + 1):
                    acc = acc + hs_ref[pl.ds(halo + ri * rc + s, rc), cols]
                y = acc * inv_cnt - hs_ref[pl.ds(halo + ri * rc, rc), cols]
                o_ref[pl.ds(ri * rc, rc), cols] = y.astype(o_ref.dtype)


def _pool_call(xs, g, modr, layer, n_rows):
    r, halo = 256, 8
    n_halo_blocks = n_rows // halo
    sh = _mod_row(layer, 0, r)
    sc = _mod_row(layer, 1, r)
    kern = functools.partial(_pool_kernel, r=r, halo=halo, n_lat_tiles=N_LAT // r)
    return pl.pallas_call(
        kern,
        grid=(n_rows // r,),
        in_specs=[
            pl.BlockSpec((r, D_MODEL), lambda i: (i, 0)),
            pl.BlockSpec((halo, D_MODEL), lambda i: (jnp.maximum(i * (r // halo) - 1, 0), 0)),
            pl.BlockSpec((halo, D_MODEL), lambda i: (jnp.minimum((i + 1) * (r // halo), n_halo_blocks - 1), 0)),
            pl.BlockSpec((1, D_MODEL), lambda i: (0, 0)),
            pl.BlockSpec((None, 1, D_MODEL), lambda i: (sh(i), 0, 0)),
            pl.BlockSpec((None, 1, D_MODEL), lambda i: (sc(i), 0, 0)),
        ],
        out_specs=pl.BlockSpec((r, D_MODEL), lambda i: (i, 0)),
        out_shape=jax.ShapeDtypeStruct((n_rows, D_MODEL), BF16),
        scratch_shapes=[pltpu.VMEM((r + 2 * halo, D_MODEL), F32)],
        compiler_params=_params(("arbitrary",), 32 << 20),
        name="norm_mod_pool",
    )(xs, xs, xs, g.reshape(1, D_MODEL), modr, modr)


def _ffn_in_kernel(a_ref, ap_ref, an_ref, w_ref, cw_ref, cb_ref, o_ref, aext_ref, u_ref, *,
                   halo, n_lat_tiles, n_dot_chunks, rc):
    i = pl.program_id(0)
    j = pl.program_id(1)
    tm, half = o_ref.shape

    @pl.when(j == 0)
    def _():
        aext_ref[0:halo, :] = ap_ref[...]
        aext_ref[halo:halo + tm, :] = a_ref[...]
        aext_ref[halo + tm:, :] = an_ref[...]

    ext = tm + 2 * halo
    dchunk = ext // n_dot_chunks
    for c in range(n_dot_chunks):
        rows = pl.ds(c * dchunk, dchunk)
        u_ref[rows, :] = jnp.dot(aext_ref[rows, :], w_ref[...], preferred_element_type=F32)

    seq_len = jnp.where(i < n_lat_tiles, SEQ, CTX_LEN)
    cw = cw_ref[...]
    cb = cb_ref[...]
    for ri in range(tm // rc):
        r0 = ri * rc
        pos = (i * tm + r0 + lax.broadcasted_iota(jnp.int32, (rc, 1), 0)) & (seq_len - 1)
        up = jnp.where(pos != 0, u_ref[pl.ds(halo + r0 - 1, rc), :], 0.0)
        un = jnp.where(pos != seq_len - 1, u_ref[pl.ds(halo + r0 + 1, rc), :], 0.0)
        uc = u_ref[pl.ds(halo + r0, rc), :]
        conv = up * cw[0:1] + uc * cw[1:2] + un * cw[2:3] + cb
        gate = conv[:, :half]
        val = conv[:, half:]
        o_ref[pl.ds(r0, rc), :] = (gate * (1.0 / (1.0 + jnp.exp(-gate))) * val).astype(o_ref.dtype)


def _ffn_in_call(h, w_il, cw_il, cb_il, n_rows):
    half, halo = 256, BF16_SUBLANES
    tn2 = 2 * half
    n_halo_blocks = n_rows // halo
    ext = TM + 2 * halo
    kern = functools.partial(_ffn_in_kernel, halo=halo, n_lat_tiles=N_LAT // TM, n_dot_chunks=2, rc=128)
    vmem = (2 * (TM * D_MODEL * 2 + D_MODEL * tn2 * 2 + TM * half * 2 + 2 * halo * D_MODEL * 2)
            + ext * D_MODEL * 2 + ext * tn2 * 4 + (12 << 20))
    return pl.pallas_call(
        kern,
        grid=(n_rows // TM, D_FF // half),
        in_specs=[
            pl.BlockSpec((TM, D_MODEL), lambda i, j: (i, 0)),
            pl.BlockSpec((halo, D_MODEL), lambda i, j: (jnp.maximum(i * (TM // halo) - 1, 0), 0)),
            pl.BlockSpec((halo, D_MODEL),
                         lambda i, j: (jnp.minimum((i + 1) * (TM // halo), n_halo_blocks - 1), 0)),
            pl.BlockSpec((D_MODEL, tn2), lambda i, j: (0, j)),
            pl.BlockSpec((3, tn2), lambda i, j: (0, j)),
            pl.BlockSpec((1, tn2), lambda i, j: (0, j)),
        ],
        out_specs=pl.BlockSpec((TM, half), lambda i, j: (i, j)),
        out_shape=jax.ShapeDtypeStruct((n_rows, D_FF), BF16),
        scratch_shapes=[
            pltpu.VMEM((ext, D_MODEL), BF16),
            pltpu.VMEM((ext, tn2), F32),
        ],
        compiler_params=_params(("arbitrary", "arbitrary"), vmem),
        name="ffn_in_conv_glu",
    )(h, h, h, w_il, cw_il, cb_il)


def _interleave_gate_value(w, half):
    lead = w.shape[:-1]
    w = w.reshape(lead + (2, D_FF // half, half))
    w = jnp.swapaxes(w, -3, -2)
    return w.reshape(lead + (2 * D_FF,))


def _rope_tables():
    rows = SEQ // GRID_W
    row = jnp.repeat(jnp.arange(rows), GRID_W).astype(F32)
    col = jnp.tile(jnp.arange(GRID_W), rows).astype(F32)
    inv = 1.0 / (ROPE_BASE ** (jnp.arange(0, ROPE_HALF, 2, dtype=F32) / ROPE_HALF))

    def tab(pos):
        a = pos[:, None] * inv[None, :]
        a = jnp.concatenate([a, a], axis=-1)
        return jnp.cos(a), jnp.sin(a)

    cr, sr = tab(row)
    cc, sc = tab(col)
    cos = jnp.concatenate([cr, cc], axis=-1)
    sin = jnp.concatenate([sr, sc], axis=-1)
    lane = jnp.arange(HEAD_DIM)
    sin = jnp.where((lane % ROPE_HALF) < ROPE_HALF // 2, -sin, sin)
    cos_all = jnp.concatenate([jnp.tile(cos, (BATCH, 1)), jnp.ones((N_CTX, HEAD_DIM), F32)], axis=0)
    sin_all = jnp.concatenate([jnp.tile(sin, (BATCH, 1)), jnp.zeros((N_CTX, HEAD_DIM), F32)], axis=0)
    return cos_all, sin_all


def _ctx_needed_after(i):
    return any((j % N_MIXERS) == 0 for j in range(i + 1, DEPTH))


def kernel(x, c, ctx, c_ctx, ada_w, ada_b, norm_g, attn_qkv, attn_o, attn_lambda, attn_subln_g,
           pool_w, pool_scale, ffn_w_in, ffn_conv_w, ffn_conv_b, ffn_w_out, final_g):
    assert x.shape == (BATCH, SEQ, D_MODEL) and ctx.shape == (BATCH, CTX_LEN, D_MODEL)
    half = 256
    xs = jnp.concatenate([x.reshape(N_LAT, D_MODEL), ctx.reshape(N_CTX, D_MODEL)], axis=0)
    cond = jnp.concatenate(
        [c, c_ctx[None, :], jnp.zeros((COND_ROWS - BATCH - 1, D_MODEL), F32)], axis=0)
    mod = _ada(cond, ada_w, ada_b)
    modr = mod.reshape(DEPTH, COND_ROWS, 6, D_MODEL).transpose(0, 2, 1, 3).reshape(
        DEPTH * 6 * COND_ROWS, 1, D_MODEL)
    cos_tab, sin_tab = _rope_tables()
    ones_row = jnp.ones((1, D_MODEL), F32)

    w_qkv = attn_qkv.astype(BF16)
    w_o = attn_o.astype(BF16)
    w_pool = pool_w.astype(BF16)
    w_in = _interleave_gate_value(ffn_w_in, half).astype(BF16)
    cw_in = _interleave_gate_value(ffn_conv_w, half)
    cb_in = _interleave_gate_value(ffn_conv_b, half).reshape(DEPTH, 1, 2 * D_FF)
    w_out = ffn_w_out.astype(BF16)

    for i in range(DEPTH):
        need_ctx = _ctx_needed_after(i)
        n_rows = N_ALL if need_ctx else N_LAT
        if i % N_MIXERS == 0:
            a = i // N_MIXERS
            lam_init = 0.8 - 0.6 * math.exp(-0.3 * i)
            h = _norm_mod_call(xs, norm_g[i, 0], modr, i, 0, N_ALL)
            qkv = _qkv_call(h, w_qkv[a], cos_tab, sin_tab, N_ALL)
            o_lat = _attn_call(qkv, attn_lambda[a], attn_subln_g[a], lam_init, latent=True)
            xs = _mm_res_call(o_lat, w_o[a], xs, modr, ones_row, i, 2, 0, tm=TM, tn=512)
            if need_ctx:
                o_ctx = _attn_call(qkv, attn_lambda[a], attn_subln_g[a], lam_init, latent=False)
                xs = _mm_res_call(o_ctx, w_o[a], xs, modr, ones_row, i, 2, N_LAT, tm=TM, tn=512)
        else:
            p = i // N_MIXERS
            y = _pool_call(xs, norm_g[i, 0], modr, i, n_rows)
            xs = _mm_res_call(y, w_pool[p], xs, modr, pool_scale[p].reshape(1, D_MODEL), i, 2, 0,
                              tm=TM, tn=POOL_GROUP, grouped=True)
        h = _norm_mod_call(xs, norm_g[i, 1], modr, i, 3, n_rows)
        act = _ffn_in_call(h, w_in[i], cw_in[i], cb_in[i], n_rows)
        xs = _mm_res_call(act, w_out[i], xs, modr, ones_row, i, 5, 0, tm=512, tn=512)
    return _final_norm(xs, final_g).reshape(BATCH, SEQ, D_MODEL)
```

```python
import functools
import math

import jax
import jax.numpy as jnp
from jax import lax
from jax.experimental import pallas as pl
from jax.experimental.pallas import tpu as pltpu

D_MODEL = 4096
BATCH = 4
SEQ = 4096
DEPTH = 4
GRID_W = 64
CTX_LEN = 256
N_MIXERS = 2
HEAD_DIM = 128
V_HEAD_DIM = 2 * HEAD_DIM
N_HEADS = D_MODEL // V_HEAD_DIM
ROPE_HALF = HEAD_DIM // 2
ROPE_BASE = 10000.0
POOL_WINDOWS = (2, 4, 8, 16)
N_POOL_GROUPS = len(POOL_WINDOWS)
POOL_GROUP = D_MODEL // N_POOL_GROUPS
D_FF = 7168
EPS = 1e-6
SUBLN_EPS = 1e-5

N_LAT = BATCH * SEQ
N_CTX = BATCH * CTX_LEN
N_ALL = N_LAT + N_CTX
COND_ROWS = 8

V7X_VMEM_BYTES = 64 * 1024 * 1024
VMEM_CAP_BYTES = V7X_VMEM_BYTES - 6 * 1024 * 1024
LANES = 128
BF16_SUBLANES = 16
LOG2_E = math.log2(math.e)

TM = 1024
BF16 = jnp.bfloat16
F32 = jnp.float32


def _params(semantics, vmem_bytes, flags=None):
    return pltpu.CompilerParams(
        dimension_semantics=semantics,
        vmem_limit_bytes=int(min(VMEM_CAP_BYTES, vmem_bytes)),
        flags=flags,
    )


def _mod_row(layer, chunk, tile_rows):
    n_lat_tiles = N_LAT // tile_rows
    tiles_per_seq = SEQ // tile_rows
    base = (layer * 6 + chunk) * COND_ROWS

    def idx(i):
        return base + jnp.where(i < n_lat_tiles, i // tiles_per_seq, BATCH)

    return idx


def _ada_kernel(cond_ref, w_ref, b_ref, o_ref):
    k = pl.program_id(2)
    c = cond_ref[...]
    c = c * (1.0 / (1.0 + jnp.exp(-c)))
    acc = jnp.dot(c.astype(BF16), w_ref[...].astype(BF16), preferred_element_type=F32)

    @pl.when(k == 0)
    def _():
        o_ref[...] = acc + b_ref[...]

    @pl.when(k != 0)
    def _():
        o_ref[...] += acc


def _ada(cond, ada_w, ada_b):
    tn, tk = 2048, 1024
    n = 6 * D_MODEL
    return pl.pallas_call(
        _ada_kernel,
        grid=(DEPTH, n // tn, D_MODEL // tk),
        in_specs=[
            pl.BlockSpec((COND_ROWS, tk), lambda i, j, k: (0, k)),
            pl.BlockSpec((None, tk, tn), lambda i, j, k: (i, k, j)),
            pl.BlockSpec((None, 1, tn), lambda i, j, k: (i, 0, j)),
        ],
        out_specs=pl.BlockSpec((None, COND_ROWS, tn), lambda i, j, k: (i, 0, j)),
        out_shape=jax.ShapeDtypeStruct((DEPTH, COND_ROWS, n), F32),
        compiler_params=_params(("arbitrary", "arbitrary", "arbitrary"), 3 * tk * tn * 4 + (8 << 20)),
        name="ada_mod",
    )(cond, ada_w, ada_b.reshape(DEPTH, 1, n))


def _norm_mod(x, g, sh, sc):
    ms = jnp.mean(x * x, axis=-1, keepdims=True)
    y = x * lax.rsqrt(ms + EPS) * g
    return y * (1.0 + sc) + sh


def _norm_mod_kernel(x_ref, g_ref, sh_ref, sc_ref, o_ref):
    o_ref[...] = _norm_mod(x_ref[...], g_ref[...], sh_ref[...], sc_ref[...]).astype(o_ref.dtype)


def _norm_mod_call(xs, g, modr, layer, chunk0, n_rows):
    r = 512
    sh = _mod_row(layer, chunk0, r)
    sc = _mod_row(layer, chunk0 + 1, r)
    return pl.pallas_call(
        _norm_mod_kernel,
        grid=(n_rows // r,),
        in_specs=[
            pl.BlockSpec((r, D_MODEL), lambda i: (i, 0)),
            pl.BlockSpec((1, D_MODEL), lambda i: (0, 0)),
            pl.BlockSpec((None, 1, D_MODEL), lambda i: (sh(i), 0, 0)),
            pl.BlockSpec((None, 1, D_MODEL), lambda i: (sc(i), 0, 0)),
        ],
        out_specs=pl.BlockSpec((r, D_MODEL), lambda i: (i, 0)),
        out_shape=jax.ShapeDtypeStruct((n_rows, D_MODEL), BF16),
        compiler_params=_params(("arbitrary",), 6 * r * D_MODEL * 4),
        name="norm_mod",
    )(xs, g.reshape(1, D_MODEL), modr, modr)


def _final_norm_kernel(x_ref, g_ref, o_ref):
    x = x_ref[...]
    ms = jnp.mean(x * x, axis=-1, keepdims=True)
    o_ref[...] = x * lax.rsqrt(ms + EPS) * g_ref[...]


def _final_norm(xs, g):
    r = 512
    return pl.pallas_call(
        _final_norm_kernel,
        grid=(N_LAT // r,),
        in_specs=[
            pl.BlockSpec((r, D_MODEL), lambda i: (i, 0)),
            pl.BlockSpec((1, D_MODEL), lambda i: (0, 0)),
        ],
        out_specs=pl.BlockSpec((r, D_MODEL), lambda i: (i, 0)),
        out_shape=jax.ShapeDtypeStruct((N_LAT, D_MODEL), F32),
        compiler_params=_params(("arbitrary",), 6 * r * D_MODEL * 4),
        name="final_norm",
    )(xs, g.reshape(1, D_MODEL))


def _qkv_kernel(a_ref, w_ref, cos_ref, sin_ref, o_ref, *, chunk, n_rope_blocks, n_q_blocks, q_scale):
    j = pl.program_id(1)
    tm, tn = o_ref.shape
    use_rope = j < n_rope_blocks
    qs = jnp.where(j < n_q_blocks, q_scale, 1.0).astype(F32)
    lane = lax.broadcasted_iota(jnp.int32, (chunk, HEAD_DIM), 1)
    first_half = (lane % ROPE_HALF) < (ROPE_HALF // 2)
    for c in range(tm // chunk):
        rows = pl.ds(c * chunk, chunk)
        acc = jnp.dot(a_ref[rows, :], w_ref[...], preferred_element_type=F32)
        cos = jnp.where(use_rope, cos_ref[rows, :], 1.0)
        sin = jnp.where(use_rope, sin_ref[rows, :], 0.0)
        for t in range(tn // HEAD_DIM):
            cols = slice(t * HEAD_DIM, (t + 1) * HEAD_DIM)
            x = acc[:, cols]
            rot = jnp.where(first_half, pltpu.roll(x, HEAD_DIM - ROPE_HALF // 2, 1),
                            pltpu.roll(x, ROPE_HALF // 2, 1))
            o_ref[rows, cols] = ((x * cos + rot * sin) * qs).astype(o_ref.dtype)


def _qkv_call(h, w, layer, cos_tab, sin_tab, n_rows):
    tn, chunk = 512, 512
    n = 3 * D_MODEL
    kern = functools.partial(
        _qkv_kernel, chunk=chunk, n_rope_blocks=2 * D_MODEL // tn, n_q_blocks=D_MODEL // tn,
        q_scale=HEAD_DIM ** -0.5 * LOG2_E)
    vmem = 2 * (TM * D_MODEL * 2 + D_MODEL * tn * 2 + TM * tn * 2 + 2 * TM * HEAD_DIM * 4) + (12 << 20)
    return pl.pallas_call(
        kern,
        grid=(n_rows // TM, n // tn),
        in_specs=[
            pl.BlockSpec((TM, D_MODEL), lambda i, j: (i, 0)),
            pl.BlockSpec((None, D_MODEL, tn), lambda i, j: (layer, 0, j)),
            pl.BlockSpec((TM, HEAD_DIM), lambda i, j: (i, 0)),
            pl.BlockSpec((TM, HEAD_DIM), lambda i, j: (i, 0)),
        ],
        out_specs=pl.BlockSpec((TM, tn), lambda i, j: (i, j)),
        out_shape=jax.ShapeDtypeStruct((n_rows, n), BF16),
        compiler_params=_params(("arbitrary", "arbitrary"), vmem),
        name="qkv_rope",
    )(h, w, cos_tab, sin_tab)


def _attn_kernel(lam_ref, g_ref, q_ref, *refs, tq, tk, n_ctx_keys, n_lat_keys, lam_init):
    if n_lat_keys:
        kc_ref, vc_ref, kl_ref, vl_ref, o_ref, m_ref, l_ref, acc_ref = refs
    else:
        kc_ref, vc_ref, o_ref, m_ref, l_ref, acc_ref = refs
    m_ref[...] = jnp.full(m_ref.shape, -jnp.inf, F32)
    l_ref[...] = jnp.zeros(l_ref.shape, F32)
    acc_ref[...] = jnp.zeros(acc_ref.shape, F32)

    def step(k, v):
        n = k.shape[0]
        for m in range(2):
            cols = slice(m * HEAD_DIM, (m + 1) * HEAD_DIM)
            s = lax.dot_general(q_ref[:, cols], k[:, cols], (((1,), (1,)), ((), ())),
                                preferred_element_type=F32)
            m_prev = m_ref[m]
            m_next = jnp.maximum(m_prev, jnp.max(s, axis=1, keepdims=True))
            alpha = jnp.exp2(m_prev - m_next)
            p = jnp.exp2(s - jnp.concatenate([m_next] * (n // LANES), axis=1))
            l_ref[m] = alpha * l_ref[m] + jnp.sum(p, axis=1, keepdims=True)
            m_ref[m] = m_next
            pv = jnp.dot(p.astype(BF16), v, preferred_element_type=F32)
            acc_ref[m] = acc_ref[m] * jnp.concatenate([alpha] * (V_HEAD_DIM // LANES), axis=1) + pv

    step(kc_ref[...], vc_ref[...])
    for c in range(n_lat_keys // tk):
        rows = pl.ds(c * tk, tk)
        step(kl_ref[rows, :], vl_ref[rows, :])

    lp = lam_ref[...]
    lam = (jnp.exp(jnp.sum(lp[0:1] * lp[1:2], axis=1, keepdims=True))
           - jnp.exp(jnp.sum(lp[2:3] * lp[3:4], axis=1, keepdims=True)) + lam_init)
    rep = V_HEAD_DIM // LANES
    o1 = acc_ref[0] / jnp.concatenate([l_ref[0]] * rep, axis=1)
    o2 = acc_ref[1] / jnp.concatenate([l_ref[1]] * rep, axis=1)
    o = o1 - lam * o2
    ms = jnp.mean(o * o, axis=-1, keepdims=True)
    y = o * lax.rsqrt(ms + SUBLN_EPS) * g_ref[...]
    o_ref[...] = (y * (1.0 - lam_init)).astype(o_ref.dtype)


def _attn_call(qkv, lam_p, subln_g, lam_init, latent):
    hq = D_MODEL // V_HEAD_DIM
    ctx_row0 = N_LAT // CTX_LEN
    if latent:
        tq, tk, n_q, n_lat_keys, n_out = 512, 1024, SEQ // 512, SEQ, N_LAT
        q_map = lambda b, h, qi: (b * (SEQ // tq) + qi, h)
    else:
        tq, tk, n_q, n_lat_keys, n_out = CTX_LEN, 512, 1, 0, N_CTX
        q_map = lambda b, h, qi: (ctx_row0 + b, h)
    in_specs = [
        pl.BlockSpec((4, HEAD_DIM), lambda b, h, qi: (0, 0)),
        pl.BlockSpec((1, V_HEAD_DIM), lambda b, h, qi: (0, 0)),
        pl.BlockSpec((tq, V_HEAD_DIM), q_map),
        pl.BlockSpec((CTX_LEN, V_HEAD_DIM), lambda b, h, qi: (ctx_row0 + b, hq + h)),
        pl.BlockSpec((CTX_LEN, V_HEAD_DIM), lambda b, h, qi: (ctx_row0 + b, 2 * hq + h)),
    ]
    args = [lam_p, subln_g.reshape(1, V_HEAD_DIM), qkv, qkv, qkv]
    if latent:
        in_specs += [
            pl.BlockSpec((SEQ, V_HEAD_DIM), lambda b, h, qi: (b, hq + h)),
            pl.BlockSpec((SEQ, V_HEAD_DIM), lambda b, h, qi: (b, 2 * hq + h)),
        ]
        args += [qkv, qkv]
        out_map = lambda b, h, qi: (b * (SEQ // tq) + qi, h)
    else:
        out_map = lambda b, h, qi: (b, h)
    kern = functools.partial(_attn_kernel, tq=tq, tk=tk, n_ctx_keys=CTX_LEN, n_lat_keys=n_lat_keys,
                             lam_init=lam_init)
    return pl.pallas_call(
        kern,
        grid=(BATCH, N_HEADS, n_q),
        in_specs=in_specs,
        out_specs=pl.BlockSpec((tq, V_HEAD_DIM), out_map),
        out_shape=jax.ShapeDtypeStruct((n_out, D_MODEL), BF16),
        scratch_shapes=[
            pltpu.VMEM((2, tq, LANES), F32),
            pltpu.VMEM((2, tq, LANES), F32),
            pltpu.VMEM((2, tq, V_HEAD_DIM), F32),
        ],
        compiler_params=_params(("arbitrary", "arbitrary", "arbitrary"), 40 << 20),
        name="diff_attn_lat" if latent else "diff_attn_ctx",
    )(*args)


def _mm_res_kernel(a_ref, w_ref, x_ref, gate_ref, cs_ref, o_ref, *, chunk):
    tm = o_ref.shape[0]
    for c in range(tm // chunk):
        rows = pl.ds(c * chunk, chunk)
        acc = jnp.dot(a_ref[rows, :], w_ref[...], preferred_element_type=F32)
        o_ref[rows, :] = x_ref[rows, :] + gate_ref[...] * (acc * cs_ref[...])


def _mm_res_call(a, w, w_idx, xs, modr, col_scale, layer, gate_chunk, row0, *, tm, tn, grouped=False):
    k = w.shape[-2]
    assert a.shape[0] % tm == 0 and row0 % tm == 0
    n_row_tiles = a.shape[0] // tm
    row_tile0 = row0 // tm
    gate = _mod_row(layer, gate_chunk, tm)
    if grouped:
        assert tn == w.shape[-1]
        a_spec = pl.BlockSpec((tm, k), lambda i, j: (i, j))
        w_spec = pl.BlockSpec((None, None, k, tn), lambda i, j: (w_idx, j, 0, 0))
    else:
        a_spec = pl.BlockSpec((tm, k), lambda i, j: (i, 0))
        w_spec = pl.BlockSpec((None, k, tn), lambda i, j: (w_idx, 0, j))
    x_spec = pl.BlockSpec((tm, tn), lambda i, j: (row_tile0 + i, j))
    vmem = 2 * (tm * k * 2 + k * tn * 2 + 2 * tm * tn * 4) + (12 << 20)
    return pl.pallas_call(
        functools.partial(_mm_res_kernel, chunk=min(tm, 512)),
        grid=(n_row_tiles, D_MODEL // tn),
        in_specs=[
            a_spec,
            w_spec,
            x_spec,
            pl.BlockSpec((None, 1, tn), lambda i, j: (gate(row_tile0 + i), 0, j)),
            pl.BlockSpec((1, tn), lambda i, j: (0, j)),
        ],
        out_specs=x_spec,
        out_shape=jax.ShapeDtypeStruct(xs.shape, F32),
        input_output_aliases={2: 0},
        compiler_params=_params(("arbitrary", "arbitrary"), vmem),
        name="mm_residual",
    )(a, w, xs, modr, col_scale)


def _pool_kernel(x_ref, xp_ref, xn_ref, g_ref, sh_ref, sc_ref, o_ref, hs_ref, *, r, halo, n_lat_tiles):
    i = pl.program_id(0)
    seq_len = jnp.where(i < n_lat_tiles, SEQ, CTX_LEN)
    pos0 = (i * r) & (seq_len - 1)
    is_first = pos0 == 0
    is_last = pos0 + r == seq_len
    g, sh, sc = g_ref[...], sh_ref[...], sc_ref[...]
    hs_ref[0:halo, :] = jnp.where(is_first, 0.0, _norm_mod(xp_ref[...], g, sh, sc))
    hs_ref[halo:halo + r, :] = _norm_mod(x_ref[...], g, sh, sc)
    hs_ref[halo + r:, :] = jnp.where(is_last, 0.0, _norm_mod(xn_ref[...], g, sh, sc))

    rc, cc = 128, 256
    for grp, w in enumerate(POOL_WINDOWS):
        before, after = w // 2, w - 1 - w // 2
        for ri in range(r // rc):
            pos = pos0 + ri * rc + lax.broadcasted_iota(jnp.int32, (rc, 1), 0)
            lo = jnp.maximum(pos - before, 0)
            hi = jnp.minimum(pos + after, seq_len - 1)
            inv_cnt = 1.0 / (hi - lo + 1).astype(F32)
            for ci in range(POOL_GROUP // cc):
                cols = slice(grp * POOL_GROUP + ci * cc, grp * POOL_GROUP + (ci + 1) * cc)
                acc = hs_ref[pl.ds(halo + ri * rc - before, rc), cols]
                for s in range(-before + 1, after + 1):
                    acc = acc + hs_ref[pl.ds(halo + ri * rc + s, rc), cols]
                y = acc * inv_cnt - hs_ref[pl.ds(halo + ri * rc, rc), cols]
                o_ref[pl.ds(ri * rc, rc), cols] = y.astype(o_ref.dtype)


def _pool_call(xs, g, modr, layer, n_rows):
    r, halo = 256, 8
    n_halo_blocks = n_rows // halo
    sh = _mod_row(layer, 0, r)
    sc = _mod_row(layer, 1, r)
    kern = functools.partial(_pool_kernel, r=r, halo=halo, n_lat_tiles=N_LAT // r)
    return pl.pallas_call(
        kern,
        grid=(n_rows // r,),
        in_specs=[
            pl.BlockSpec((r, D_MODEL), lambda i: (i, 0)),
            pl.BlockSpec((halo, D_MODEL), lambda i: (jnp.maximum(i * (r // halo) - 1, 0), 0)),
            pl.BlockSpec((halo, D_MODEL), lambda i: (jnp.minimum((i + 1) * (r // halo), n_halo_blocks - 1), 0)),
            pl.BlockSpec((1, D_MODEL), lambda i: (0, 0)),
            pl.BlockSpec((None, 1, D_MODEL), lambda i: (sh(i), 0, 0)),
            pl.BlockSpec((None, 1, D_MODEL), lambda i: (sc(i), 0, 0)),
        ],
        out_specs=pl.BlockSpec((r, D_MODEL), lambda i: (i, 0)),
        out_shape=jax.ShapeDtypeStruct((n_rows, D_MODEL), BF16),
        scratch_shapes=[pltpu.VMEM((r + 2 * halo, D_MODEL), F32)],
        compiler_params=_params(("arbitrary",), 32 << 20),
        name="norm_mod_pool",
    )(xs, xs, xs, g.reshape(1, D_MODEL), modr, modr)


def _ffn_in_kernel(a_ref, ap_ref, an_ref, wg_ref, wv_ref, cwg_ref, cwv_ref, cbg_ref, cbv_ref, o_ref,
                   aext_ref, u_ref, *, halo, n_lat_tiles, dot_chunks, rc):
    i = pl.program_id(0)
    j = pl.program_id(1)
    tm, half = o_ref.shape

    @pl.when(j == 0)
    def _():
        aext_ref[0:halo, :] = ap_ref[...]
        aext_ref[halo:halo + tm, :] = a_ref[...]
        aext_ref[halo + tm:, :] = an_ref[...]

    row = 0
    for size in dot_chunks:
        rows = pl.ds(row, size)
        a = aext_ref[rows, :]
        u_ref[rows, 0:half] = jnp.dot(a, wg_ref[...], preferred_element_type=F32)
        u_ref[rows, half:] = jnp.dot(a, wv_ref[...], preferred_element_type=F32)
        row += size

    seq_len = jnp.where(i < n_lat_tiles, SEQ, CTX_LEN)
    cw = jnp.concatenate([cwg_ref[...], cwv_ref[...]], axis=1)
    cb = jnp.concatenate([cbg_ref[...], cbv_ref[...]], axis=1)
    sub = lax.broadcasted_iota(jnp.int32, (8, 1), 0)
    for ri in range(tm // rc):
        r0 = ri * rc
        up = u_ref[pl.ds(halo + r0 - 1, rc), :]
        un = u_ref[pl.ds(halo + r0 + 1, rc), :]
        uc = u_ref[pl.ds(halo + r0, rc), :]
        if r0 % CTX_LEN == 0:
            pos = (i * tm + r0 + sub) & (seq_len - 1)
            up = jnp.concatenate([jnp.where(pos != 0, up[0:8], 0.0), up[8:]], axis=0)
        if (r0 + rc) % CTX_LEN == 0:
            pos = (i * tm + r0 + rc - 8 + sub) & (seq_len - 1)
            un = jnp.concatenate([un[:rc - 8], jnp.where(pos != seq_len - 1, un[rc - 8:], 0.0)], axis=0)
        conv = up * cw[0:1] + uc * cw[1:2] + un * cw[2:3] + cb
        gate = conv[:, :half]
        val = conv[:, half:]
        o_ref[pl.ds(r0, rc), :] = (gate * (1.0 / (1.0 + jnp.exp(-gate))) * val).astype(o_ref.dtype)


def _ffn_in_call(h, w_in, conv_w, conv_b, layer, n_rows):
    half, halo = 256, BF16_SUBLANES
    n_halo_blocks = n_rows // halo
    ext = TM + 2 * halo
    nj = D_FF // half
    kern = functools.partial(_ffn_in_kernel, halo=halo, n_lat_tiles=N_LAT // TM,
                             dot_chunks=(272, 272, 256, 256), rc=64)
    vmem = (2 * (TM * D_MODEL * 2 + 2 * D_MODEL * half * 2 + TM * half * 2 + 2 * halo * D_MODEL * 2)
            + ext * D_MODEL * 2 + ext * 2 * half * 4 + (12 << 20))
    return pl.pallas_call(
        kern,
        grid=(n_rows // TM, nj),
        in_specs=[
            pl.BlockSpec((TM, D_MODEL), lambda i, j: (i, 0)),
            pl.BlockSpec((halo, D_MODEL), lambda i, j: (jnp.maximum(i * (TM // halo) - 1, 0), 0)),
            pl.BlockSpec((halo, D_MODEL),
                         lambda i, j: (jnp.minimum((i + 1) * (TM // halo), n_halo_blocks - 1), 0)),
            pl.BlockSpec((None, D_MODEL, half), lambda i, j: (layer, 0, j)),
            pl.BlockSpec((None, D_MODEL, half), lambda i, j: (layer, 0, nj + j)),
            pl.BlockSpec((None, 3, half), lambda i, j: (layer, 0, j)),
            pl.BlockSpec((None, 3, half), lambda i, j: (layer, 0, nj + j)),
            pl.BlockSpec((None, 1, half), lambda i, j: (layer, 0, j)),
            pl.BlockSpec((None, 1, half), lambda i, j: (layer, 0, nj + j)),
        ],
        out_specs=pl.BlockSpec((TM, half), lambda i, j: (i, j)),
        out_shape=jax.ShapeDtypeStruct((n_rows, D_FF), BF16),
        scratch_shapes=[
            pltpu.VMEM((ext, D_MODEL), BF16),
            pltpu.VMEM((ext, 2 * half), F32),
        ],
        compiler_params=_params(("arbitrary", "arbitrary"), vmem),
        name="ffn_in_conv_glu",
    )(h, h, h, w_in, w_in, conv_w, conv_w, conv_b, conv_b)


def _rope_tables():
    rows = SEQ // GRID_W
    row = jnp.repeat(jnp.arange(rows), GRID_W).astype(F32)
    col = jnp.tile(jnp.arange(GRID_W), rows).astype(F32)
    inv = 1.0 / (ROPE_BASE ** (jnp.arange(0, ROPE_HALF, 2, dtype=F32) / ROPE_HALF))

    def tab(pos):
        a = pos[:, None] * inv[None, :]
        a = jnp.concatenate([a, a], axis=-1)
        return jnp.cos(a), jnp.sin(a)

    cr, sr = tab(row)
    cc, sc = tab(col)
    cos = jnp.concatenate([cr, cc], axis=-1)
    sin = jnp.concatenate([sr, sc], axis=-1)
    lane = jnp.arange(HEAD_DIM)
    sin = jnp.where((lane % ROPE_HALF) < ROPE_HALF // 2, -sin, sin)
    cos_all = jnp.concatenate([jnp.tile(cos, (BATCH, 1)), jnp.ones((N_CTX, HEAD_DIM), F32)], axis=0)
    sin_all = jnp.concatenate([jnp.tile(sin, (BATCH, 1)), jnp.zeros((N_CTX, HEAD_DIM), F32)], axis=0)
    return cos_all, sin_all


def _ctx_needed_after(i):
    return any((j % N_MIXERS) == 0 for j in range(i + 1, DEPTH))


def kernel(x, c, ctx, c_ctx, ada_w, ada_b, norm_g, attn_qkv, attn_o, attn_lambda, attn_subln_g,
           pool_w, pool_scale, ffn_w_in, ffn_conv_w, ffn_conv_b, ffn_w_out, final_g):
    assert x.shape == (BATCH, SEQ, D_MODEL) and ctx.shape == (BATCH, CTX_LEN, D_MODEL)
    xs = jnp.concatenate([x.reshape(N_LAT, D_MODEL), ctx.reshape(N_CTX, D_MODEL)], axis=0)
    cond = jnp.concatenate(
        [c, c_ctx[None, :], jnp.zeros((COND_ROWS - BATCH - 1, D_MODEL), F32)], axis=0)
    mod = _ada(cond, ada_w, ada_b)
    modr = mod.reshape(DEPTH, COND_ROWS, 6, D_MODEL).transpose(0, 2, 1, 3).reshape(
        DEPTH * 6 * COND_ROWS, 1, D_MODEL)
    cos_tab, sin_tab = _rope_tables()
    ones_row = jnp.ones((1, D_MODEL), F32)

    w_qkv = attn_qkv.astype(BF16)
    w_o = attn_o.astype(BF16)
    w_pool = pool_w.astype(BF16)
    w_in = ffn_w_in.astype(BF16)
    conv_b = ffn_conv_b.reshape(DEPTH, 1, 2 * D_FF)
    w_out = ffn_w_out.astype(BF16)

    for i in range(DEPTH):
        need_ctx = _ctx_needed_after(i)
        n_rows = N_ALL if need_ctx else N_LAT
        if i % N_MIXERS == 0:
            a = i // N_MIXERS
            lam_init = 0.8 - 0.6 * math.exp(-0.3 * i)
            h = _norm_mod_call(xs, norm_g[i, 0], modr, i, 0, N_ALL)
            qkv = _qkv_call(h, w_qkv, a, cos_tab, sin_tab, N_ALL)
            o_lat = _attn_call(qkv, attn_lambda[a], attn_subln_g[a], lam_init, latent=True)
            xs = _mm_res_call(o_lat, w_o, a, xs, modr, ones_row, i, 2, 0, tm=TM, tn=512)
            if need_ctx:
                o_ctx = _attn_call(qkv, attn_lambda[a], attn_subln_g[a], lam_init, latent=False)
                xs = _mm_res_call(o_ctx, w_o, a, xs, modr, ones_row, i, 2, N_LAT, tm=TM, tn=512)
        else:
            p = i // N_MIXERS
            y = _pool_call(xs, norm_g[i, 0], modr, i, n_rows)
            xs = _mm_res_call(y, w_pool, p, xs, modr, pool_scale[p].reshape(1, D_MODEL), i, 2, 0,
                              tm=TM, tn=POOL_GROUP, grouped=True)
        h = _norm_mod_call(xs, norm_g[i, 1], modr, i, 3, n_rows)
        act = _ffn_in_call(h, w_in, ffn_conv_w, conv_b, i, n_rows)
        xs = _mm_res_call(act, w_out, i, xs, modr, ones_row, i, 5, 0, tm=512, tn=512)
    return _final_norm(xs, final_g).reshape(BATCH, SEQ, D_MODEL)
```

```python
import functools
import math

import jax
import jax.numpy as jnp
from jax import lax
from jax.experimental import pallas as pl
from jax.experimental.pallas import tpu as pltpu

D_MODEL = 4096
BATCH = 4
SEQ = 4096
DEPTH = 4
GRID_W = 64
CTX_LEN = 256
N_MIXERS = 2
HEAD_DIM = 128
V_HEAD_DIM = 2 * HEAD_DIM
N_HEADS = D_MODEL // V_HEAD_DIM
ROPE_HALF = HEAD_DIM // 2
ROPE_BASE = 10000.0
POOL_WINDOWS = (2, 4, 8, 16)
N_POOL_GROUPS = len(POOL_WINDOWS)
POOL_GROUP = D_MODEL // N_POOL_GROUPS
D_FF = 7168
EPS = 1e-6
SUBLN_EPS = 1e-5

N_LAT = BATCH * SEQ
N_CTX = BATCH * CTX_LEN
N_ALL = N_LAT + N_CTX
COND_ROWS = 8

V7X_VMEM_BYTES = 64 * 1024 * 1024
VMEM_CAP_BYTES = V7X_VMEM_BYTES - 6 * 1024 * 1024
LANES = 128
BF16_SUBLANES = 16
LOG2_E = math.log2(math.e)

TM = 1024
BF16 = jnp.bfloat16
F32 = jnp.float32


def _params(semantics, vmem_bytes, flags=None):
    return pltpu.CompilerParams(
        dimension_semantics=semantics,
        vmem_limit_bytes=int(min(VMEM_CAP_BYTES, vmem_bytes)),
        flags=flags,
    )


def _mod_row(layer, chunk, tile_rows):
    n_lat_tiles = N_LAT // tile_rows
    tiles_per_seq = SEQ // tile_rows
    base = (layer * 6 + chunk) * COND_ROWS

    def idx(i):
        return base + jnp.where(i < n_lat_tiles, i // tiles_per_seq, BATCH)

    return idx


def _ada_kernel(cond_ref, w_ref, b_ref, o_ref):
    k = pl.program_id(2)
    c = cond_ref[...]
    c = c * (1.0 / (1.0 + jnp.exp(-c)))
    acc = jnp.dot(c.astype(BF16), w_ref[...].astype(BF16), preferred_element_type=F32)

    @pl.when(k == 0)
    def _():
        o_ref[...] = acc + b_ref[...]

    @pl.when(k != 0)
    def _():
        o_ref[...] += acc


def _ada(cond, ada_w, ada_b):
    tn, tk = 2048, 1024
    n = 6 * D_MODEL
    return pl.pallas_call(
        _ada_kernel,
        grid=(DEPTH, n // tn, D_MODEL // tk),
        in_specs=[
            pl.BlockSpec((COND_ROWS, tk), lambda i, j, k: (0, k)),
            pl.BlockSpec((None, tk, tn), lambda i, j, k: (i, k, j)),
            pl.BlockSpec((None, 1, tn), lambda i, j, k: (i, 0, j)),
        ],
        out_specs=pl.BlockSpec((None, COND_ROWS, tn), lambda i, j, k: (i, 0, j)),
        out_shape=jax.ShapeDtypeStruct((DEPTH, COND_ROWS, n), F32),
        compiler_params=_params(("arbitrary", "arbitrary", "arbitrary"), 3 * tk * tn * 4 + (8 << 20)),
        name="ada_mod",
    )(cond, ada_w, ada_b.reshape(DEPTH, 1, n))


def _norm_mod(x, g, sh, sc):
    ms = jnp.mean(x * x, axis=-1, keepdims=True)
    y = x * lax.rsqrt(ms + EPS) * g
    return y * (1.0 + sc) + sh


def _norm_mod_kernel(x_ref, g_ref, sh_ref, sc_ref, o_ref):
    o_ref[...] = _norm_mod(x_ref[...], g_ref[...], sh_ref[...], sc_ref[...]).astype(o_ref.dtype)


def _norm_mod_call(xs, g, modr, layer, chunk0, n_rows):
    r = 512
    sh = _mod_row(layer, chunk0, r)
    sc = _mod_row(layer, chunk0 + 1, r)
    return pl.pallas_call(
        _norm_mod_kernel,
        grid=(n_rows // r,),
        in_specs=[
            pl.BlockSpec((r, D_MODEL), lambda i: (i, 0)),
            pl.BlockSpec((1, D_MODEL), lambda i: (0, 0)),
            pl.BlockSpec((None, 1, D_MODEL), lambda i: (sh(i), 0, 0)),
            pl.BlockSpec((None, 1, D_MODEL), lambda i: (sc(i), 0, 0)),
        ],
        out_specs=pl.BlockSpec((r, D_MODEL), lambda i: (i, 0)),
        out_shape=jax.ShapeDtypeStruct((n_rows, D_MODEL), BF16),
        compiler_params=_params(("arbitrary",), 6 * r * D_MODEL * 4),
        name="norm_mod",
    )(xs, g.reshape(1, D_MODEL), modr, modr)


def _final_norm_kernel(x_ref, g_ref, o_ref):
    x = x_ref[...]
    ms = jnp.mean(x * x, axis=-1, keepdims=True)
    o_ref[...] = x * lax.rsqrt(ms + EPS) * g_ref[...]


def _final_norm(xs, g):
    r = 512
    return pl.pallas_call(
        _final_norm_kernel,
        grid=(N_LAT // r,),
        in_specs=[
            pl.BlockSpec((r, D_MODEL), lambda i: (i, 0)),
            pl.BlockSpec((1, D_MODEL), lambda i: (0, 0)),
        ],
        out_specs=pl.BlockSpec((r, D_MODEL), lambda i: (i, 0)),
        out_shape=jax.ShapeDtypeStruct((N_LAT, D_MODEL), F32),
        compiler_params=_params(("arbitrary",), 6 * r * D_MODEL * 4),
        name="final_norm",
    )(xs, g.reshape(1, D_MODEL))


def _qkv_kernel(a_ref, w_ref, cos_ref, sin_ref, o_ref, u0_ref, u1_ref, *, n_col_tiles, dot_chunks, rc,
                n_rope_blocks, n_q_blocks, q_scale):
    t = pl.program_id(0)
    tm, tn = o_ref.shape
    je = jnp.maximum(t - 1, 0) % n_col_tiles
    use_rope = je < n_rope_blocks
    qs = jnp.where(je < n_q_blocks, q_scale, 1.0).astype(F32)
    lane = lax.broadcasted_iota(jnp.int32, (rc, HEAD_DIM), 1)
    first_half = (lane % ROPE_HALF) < (ROPE_HALF // 2)

    @pl.when(t == 0)
    def _():
        u1_ref[...] = jnp.zeros(u1_ref.shape, F32)

    def work(uw_ref, ur_ref):
        per_chunk = (tm // rc) // len(dot_chunks)
        row = 0
        for c, size in enumerate(dot_chunks):
            for ri in range(c * per_chunk, (c + 1) * per_chunk):
                rows = pl.ds(ri * rc, rc)
                cos = jnp.where(use_rope, cos_ref[rows, :], 1.0)
                sin = jnp.where(use_rope, sin_ref[rows, :], 0.0)
                for ci in range(tn // HEAD_DIM):
                    cols = slice(ci * HEAD_DIM, (ci + 1) * HEAD_DIM)
                    x = ur_ref[rows, cols]
                    rot = jnp.where(first_half, pltpu.roll(x, HEAD_DIM - ROPE_HALF // 2, 1),
                                    pltpu.roll(x, ROPE_HALF // 2, 1))
                    o_ref[rows, cols] = ((x * cos + rot * sin) * qs).astype(o_ref.dtype)
            rows = pl.ds(row, size)
            uw_ref[rows, :] = jnp.dot(a_ref[rows, :], w_ref[...], preferred_element_type=F32)
            row += size

    @pl.when(t % 2 == 0)
    def _():
        work(u0_ref, u1_ref)

    @pl.when(t % 2 == 1)
    def _():
        work(u1_ref, u0_ref)


def _qkv_call(h, w, layer, cos_tab, sin_tab, n_rows):
    tn = 512
    n = 3 * D_MODEL
    nj = n // tn
    n_steps = (n_rows // TM) * nj
    kern = functools.partial(
        _qkv_kernel, n_col_tiles=nj, dot_chunks=(256, 256, 256, 256), rc=64,
        n_rope_blocks=2 * D_MODEL // tn, n_q_blocks=D_MODEL // tn, q_scale=HEAD_DIM ** -0.5 * LOG2_E)
    vmem = (2 * (TM * D_MODEL * 2 + D_MODEL * tn * 2 + TM * tn * 2 + 2 * TM * HEAD_DIM * 4)
            + 2 * TM * tn * 4 + (12 << 20))

    def mm(t):
        tt = jnp.minimum(t, n_steps - 1)
        return tt // nj, tt % nj

    def ep(t):
        te = jnp.maximum(t - 1, 0)
        return te // nj, te % nj

    return pl.pallas_call(
        kern,
        grid=(n_steps + 1,),
        in_specs=[
            pl.BlockSpec((TM, D_MODEL), lambda t: (mm(t)[0], 0)),
            pl.BlockSpec((None, D_MODEL, tn), lambda t: (layer, 0, mm(t)[1])),
            pl.BlockSpec((TM, HEAD_DIM), lambda t: (ep(t)[0], 0)),
            pl.BlockSpec((TM, HEAD_DIM), lambda t: (ep(t)[0], 0)),
        ],
        out_specs=pl.BlockSpec((TM, tn), lambda t: ep(t)),
        out_shape=jax.ShapeDtypeStruct((n_rows, n), BF16),
        scratch_shapes=[pltpu.VMEM((TM, tn), F32), pltpu.VMEM((TM, tn), F32)],
        compiler_params=_params(("arbitrary",), vmem),
        name="qkv_rope",
    )(h, w, cos_tab, sin_tab)


def _attn_kernel(lam_ref, g_ref, q_ref, *refs, tq, tk, n_ctx_keys, n_lat_keys, lam_init):
    if n_lat_keys:
        kc_ref, vc_ref, kl_ref, vl_ref, o_ref, m_ref, l_ref, acc_ref = refs
    else:
        kc_ref, vc_ref, o_ref, m_ref, l_ref, acc_ref = refs
    m_ref[...] = jnp.full(m_ref.shape, -jnp.inf, F32)
    l_ref[...] = jnp.zeros(l_ref.shape, F32)
    acc_ref[...] = jnp.zeros(acc_ref.shape, F32)

    def scores(k):
        out = []
        for m in range(2):
            cols = slice(m * HEAD_DIM, (m + 1) * HEAD_DIM)
            out.append(lax.dot_general(q_ref[:, cols], k[:, cols], (((1,), (1,)), ((), ())),
                                       preferred_element_type=F32))
        return out

    def softmax_pv(s_maps, v):
        for m, s in enumerate(s_maps):
            n = s.shape[1]
            m_prev = m_ref[m]
            m_next = jnp.maximum(m_prev, jnp.max(s, axis=1, keepdims=True))
            alpha = jnp.exp2(m_prev - m_next)
            p = jnp.exp2(s - jnp.concatenate([m_next] * (n // LANES), axis=1))
            l_ref[m] = alpha * l_ref[m] + jnp.sum(p, axis=1, keepdims=True)
            m_ref[m] = m_next
            pv = jnp.dot(p.astype(BF16), v, preferred_element_type=F32)
            acc_ref[m] = acc_ref[m] * jnp.concatenate([alpha] * (V_HEAD_DIM // LANES), axis=1) + pv

    chunks = [(kc_ref, vc_ref, pl.ds(0, n_ctx_keys))]
    chunks += [(kl_ref, vl_ref, pl.ds(c * tk, tk)) for c in range(n_lat_keys // tk)]
    s_next = scores(chunks[0][0][chunks[0][2], :])
    for c, (_, v_ref, rows) in enumerate(chunks):
        s_cur = s_next
        if c + 1 < len(chunks):
            k_next, _, rows_next = chunks[c + 1]
            s_next = scores(k_next[rows_next, :])
        softmax_pv(s_cur, v_ref[rows, :])

    lp = lam_ref[...]
    lam = (jnp.exp(jnp.sum(lp[0:1] * lp[1:2], axis=1, keepdims=True))
           - jnp.exp(jnp.sum(lp[2:3] * lp[3:4], axis=1, keepdims=True)) + lam_init)
    rep = V_HEAD_DIM // LANES
    o1 = acc_ref[0] / jnp.concatenate([l_ref[0]] * rep, axis=1)
    o2 = acc_ref[1] / jnp.concatenate([l_ref[1]] * rep, axis=1)
    o = o1 - lam * o2
    ms = jnp.mean(o * o, axis=-1, keepdims=True)
    y = o * lax.rsqrt(ms + SUBLN_EPS) * g_ref[...]
    o_ref[...] = (y * (1.0 - lam_init)).astype(o_ref.dtype)


def _attn_call(qkv, lam_p, subln_g, lam_init, latent):
    hq = D_MODEL // V_HEAD_DIM
    ctx_row0 = N_LAT // CTX_LEN
    if latent:
        tq, tk, n_lat_keys, n_out = 512, 1024, SEQ, N_LAT
        n_q = SEQ // tq
        q_map = lambda b, h, qi: (b * (SEQ // tq) + qi, h)
    else:
        tq, tk, n_q, n_lat_keys, n_out = CTX_LEN, 512, 1, 0, N_CTX
        q_map = lambda b, h, qi: (ctx_row0 + b, h)
    in_specs = [
        pl.BlockSpec((4, HEAD_DIM), lambda b, h, qi: (0, 0)),
        pl.BlockSpec((1, V_HEAD_DIM), lambda b, h, qi: (0, 0)),
        pl.BlockSpec((tq, V_HEAD_DIM), q_map),
        pl.BlockSpec((CTX_LEN, V_HEAD_DIM), lambda b, h, qi: (ctx_row0 + b, hq + h)),
        pl.BlockSpec((CTX_LEN, V_HEAD_DIM), lambda b, h, qi: (ctx_row0 + b, 2 * hq + h)),
    ]
    args = [lam_p, subln_g.reshape(1, V_HEAD_DIM), qkv, qkv, qkv]
    if latent:
        in_specs += [
            pl.BlockSpec((SEQ, V_HEAD_DIM), lambda b, h, qi: (b, hq + h)),
            pl.BlockSpec((SEQ, V_HEAD_DIM), lambda b, h, qi: (b, 2 * hq + h)),
        ]
        args += [qkv, qkv]
        out_map = lambda b, h, qi: (b * (SEQ // tq) + qi, h)
    else:
        out_map = lambda b, h, qi: (b, h)
    kern = functools.partial(_attn_kernel, tq=tq, tk=tk, n_ctx_keys=CTX_LEN, n_lat_keys=n_lat_keys,
                             lam_init=lam_init)
    return pl.pallas_call(
        kern,
        grid=(BATCH, N_HEADS, n_q),
        in_specs=in_specs,
        out_specs=pl.BlockSpec((tq, V_HEAD_DIM), out_map),
        out_shape=jax.ShapeDtypeStruct((n_out, D_MODEL), BF16),
        scratch_shapes=[
            pltpu.VMEM((2, tq, LANES), F32),
            pltpu.VMEM((2, tq, LANES), F32),
            pltpu.VMEM((2, tq, V_HEAD_DIM), F32),
        ],
        compiler_params=_params(("arbitrary", "arbitrary", "arbitrary"), 40 << 20),
        name="diff_attn_lat" if latent else "diff_attn_ctx",
    )(*args)


def _mm_res_kernel(a_ref, w_ref, x_ref, gate_ref, cs_ref, o_ref, *, chunk):
    tm = o_ref.shape[0]
    for c in range(tm // chunk):
        rows = pl.ds(c * chunk, chunk)
        acc = jnp.dot(a_ref[rows, :], w_ref[...], preferred_element_type=F32)
        o_ref[rows, :] = x_ref[rows, :] + gate_ref[...] * (acc * cs_ref[...])


def _mm_res_call(a, w, w_idx, xs, modr, col_scale, layer, gate_chunk, row0, *, tm, tn, grouped=False):
    k = w.shape[-2]
    assert a.shape[0] % tm == 0 and row0 % tm == 0
    n_row_tiles = a.shape[0] // tm
    row_tile0 = row0 // tm
    gate = _mod_row(layer, gate_chunk, tm)
    if grouped:
        assert tn == w.shape[-1]
        a_spec = pl.BlockSpec((tm, k), lambda i, j: (i, j))
        w_spec = pl.BlockSpec((None, None, k, tn), lambda i, j: (w_idx, j, 0, 0))
    else:
        a_spec = pl.BlockSpec((tm, k), lambda i, j: (i, 0))
        w_spec = pl.BlockSpec((None, k, tn), lambda i, j: (w_idx, 0, j))
    x_spec = pl.BlockSpec((tm, tn), lambda i, j: (row_tile0 + i, j))
    vmem = 2 * (tm * k * 2 + k * tn * 2 + 2 * tm * tn * 4) + (12 << 20)
    return pl.pallas_call(
        functools.partial(_mm_res_kernel, chunk=min(tm, 512)),
        grid=(n_row_tiles, D_MODEL // tn),
        in_specs=[
            a_spec,
            w_spec,
            x_spec,
            pl.BlockSpec((None, 1, tn), lambda i, j: (gate(row_tile0 + i), 0, j)),
            pl.BlockSpec((1, tn), lambda i, j: (0, j)),
        ],
        out_specs=x_spec,
        out_shape=jax.ShapeDtypeStruct(xs.shape, F32),
        input_output_aliases={2: 0},
        compiler_params=_params(("arbitrary", "arbitrary"), vmem),
        name="mm_residual",
    )(a, w, xs, modr, col_scale)


def _pool_kernel(x_ref, xp_ref, xn_ref, g_ref, sh_ref, sc_ref, o_ref, hs_ref, *, r, halo, n_lat_tiles):
    i = pl.program_id(0)
    seq_len = jnp.where(i < n_lat_tiles, SEQ, CTX_LEN)
    pos0 = (i * r) & (seq_len - 1)
    is_first = pos0 == 0
    is_last = pos0 + r == seq_len
    g, sh, sc = g_ref[...], sh_ref[...], sc_ref[...]
    hs_ref[0:halo, :] = jnp.where(is_first, 0.0, _norm_mod(xp_ref[...], g, sh, sc))
    hs_ref[halo:halo + r, :] = _norm_mod(x_ref[...], g, sh, sc)
    hs_ref[halo + r:, :] = jnp.where(is_last, 0.0, _norm_mod(xn_ref[...], g, sh, sc))

    rc, cc = 128, 256
    for grp, w in enumerate(POOL_WINDOWS):
        before, after = w // 2, w - 1 - w // 2
        for ri in range(r // rc):
            pos = pos0 + ri * rc + lax.broadcasted_iota(jnp.int32, (rc, 1), 0)
            lo = jnp.maximum(pos - before, 0)
            hi = jnp.minimum(pos + after, seq_len - 1)
            inv_cnt = 1.0 / (hi - lo + 1).astype(F32)
            for ci in range(POOL_GROUP // cc):
                cols = slice(grp * POOL_GROUP + ci * cc, grp * POOL_GROUP + (ci + 1) * cc)
                acc = hs_ref[pl.ds(halo + ri * rc - before, rc), cols]
                for s in range(-before + 1, after + 1):
                    acc = acc + hs_ref[pl.ds(halo + ri * rc + s, rc), cols]
                y = acc * inv_cnt - hs_ref[pl.ds(halo + ri * rc, rc), cols]
                o_ref[pl.ds(ri * rc, rc), cols] = y.astype(o_ref.dtype)


def _pool_call(xs, g, modr, layer, n_rows):
    r, halo = 256, 8
    n_halo_blocks = n_rows // halo
    sh = _mod_row(layer, 0, r)
    sc = _mod_row(layer, 1, r)
    kern = functools.partial(_pool_kernel, r=r, halo=halo, n_lat_tiles=N_LAT // r)
    return pl.pallas_call(
        kern,
        grid=(n_rows // r,),
        in_specs=[
            pl.BlockSpec((r, D_MODEL), lambda i: (i, 0)),
            pl.BlockSpec((halo, D_MODEL), lambda i: (jnp.maximum(i * (r // halo) - 1, 0), 0)),
            pl.BlockSpec((halo, D_MODEL), lambda i: (jnp.minimum((i + 1) * (r // halo), n_halo_blocks - 1), 0)),
            pl.BlockSpec((1, D_MODEL), lambda i: (0, 0)),
            pl.BlockSpec((None, 1, D_MODEL), lambda i: (sh(i), 0, 0)),
            pl.BlockSpec((None, 1, D_MODEL), lambda i: (sc(i), 0, 0)),
        ],
        out_specs=pl.BlockSpec((r, D_MODEL), lambda i: (i, 0)),
        out_shape=jax.ShapeDtypeStruct((n_rows, D_MODEL), BF16),
        scratch_shapes=[pltpu.VMEM((r + 2 * halo, D_MODEL), F32)],
        compiler_params=_params(("arbitrary",), 32 << 20),
        name="norm_mod_pool",
    )(xs, xs, xs, g.reshape(1, D_MODEL), modr, modr)


def _ffn_in_kernel(a_ref, ap_ref, an_ref, wg_ref, wv_ref, cwg_ref, cwv_ref, cbg_ref, cbv_ref, o_ref,
                   aext_ref, u0_ref, u1_ref, *, halo, n_col_tiles, n_lat_tiles, dot_chunks, rc):
    t = pl.program_id(0)
    tm, half = o_ref.shape
    j = jnp.minimum(t, pl.num_programs(0) - 2) % n_col_tiles

    @pl.when(t == 0)
    def _():
        u1_ref[...] = jnp.zeros(u1_ref.shape, F32)

    @pl.when(j == 0)
    def _():
        aext_ref[0:halo, :] = ap_ref[...]
        aext_ref[halo:halo + tm, :] = a_ref[...]
        aext_ref[halo + tm:, :] = an_ref[...]

    ie = jnp.maximum(t - 1, 0) // n_col_tiles
    seq_len = jnp.where(ie < n_lat_tiles, SEQ, CTX_LEN)
    sub = lax.broadcasted_iota(jnp.int32, (8, 1), 0)

    def conv(ur_ref, cols, r0, cw_ref, cb_ref, ccols):
        up = ur_ref[pl.ds(halo + r0 - 1, rc), cols]
        un = ur_ref[pl.ds(halo + r0 + 1, rc), cols]
        uc = ur_ref[pl.ds(halo + r0, rc), cols]
        if r0 % CTX_LEN == 0:
            pos = (ie * tm + r0 + sub) & (seq_len - 1)
            up = jnp.concatenate([jnp.where(pos != 0, up[0:8], 0.0), up[8:]], axis=0)
        if (r0 + rc) % CTX_LEN == 0:
            pos = (ie * tm + r0 + rc - 8 + sub) & (seq_len - 1)
            un = jnp.concatenate([un[:rc - 8], jnp.where(pos != seq_len - 1, un[rc - 8:], 0.0)], axis=0)
        return (up * cw_ref[0:1, ccols] + uc * cw_ref[1:2, ccols] + un * cw_ref[2:3, ccols]
                + cb_ref[:, ccols])

    def work(uw_ref, ur_ref):
        n_pieces = tm // rc
        per_chunk = n_pieces // len(dot_chunks)
        row = 0
        for c, size in enumerate(dot_chunks):
            for ri in range(c * per_chunk, (c + 1) * per_chunk):
                r0 = ri * rc
                for ci in range(half // LANES):
                    ccols = slice(ci * LANES, (ci + 1) * LANES)
                    gate = conv(ur_ref, ccols, r0, cwg_ref, cbg_ref, ccols)
                    val = conv(ur_ref, slice(half + ci * LANES, half + (ci + 1) * LANES), r0, cwv_ref,
                               cbv_ref, ccols)
                    o_ref[pl.ds(r0, rc), ccols] = (
                        gate * (1.0 / (1.0 + jnp.exp(-gate))) * val).astype(o_ref.dtype)
            rows = pl.ds(row, size)
            a = aext_ref[rows, :]
            uw_ref[rows, 0:half] = jnp.dot(a, wg_ref[...], preferred_element_type=F32)
            uw_ref[rows, half:] = jnp.dot(a, wv_ref[...], preferred_element_type=F32)
            row += size

    @pl.when(t % 2 == 0)
    def _():
        work(u0_ref, u1_ref)

    @pl.when(t % 2 == 1)
    def _():
        work(u1_ref, u0_ref)


def _ffn_in_call(h, w_in, conv_w, conv_b, layer, n_rows):
    half, halo = 256, BF16_SUBLANES
    n_halo_blocks = n_rows // halo
    ext = TM + 2 * halo
    nj = D_FF // half
    n_steps = (n_rows // TM) * nj
    kern = functools.partial(_ffn_in_kernel, halo=halo, n_col_tiles=nj, n_lat_tiles=N_LAT // TM,
                             dot_chunks=(272, 272, 256, 256), rc=32)
    vmem = (2 * (TM * D_MODEL * 2 + 2 * D_MODEL * half * 2 + TM * half * 2 + 2 * halo * D_MODEL * 2)
            + ext * D_MODEL * 2 + 2 * ext * 2 * half * 4 + (12 << 20))

    def mm(t):
        tt = jnp.minimum(t, n_steps - 1)
        return tt // nj, tt % nj

    def ep(t):
        te = jnp.maximum(t - 1, 0)
        return te // nj, te % nj

    return pl.pallas_call(
        kern,
        grid=(n_steps + 1,),
        in_specs=[
            pl.BlockSpec((TM, D_MODEL), lambda t: (mm(t)[0], 0)),
            pl.BlockSpec((halo, D_MODEL), lambda t: (jnp.maximum(mm(t)[0] * (TM // halo) - 1, 0), 0)),
            pl.BlockSpec((halo, D_MODEL),
                         lambda t: (jnp.minimum((mm(t)[0] + 1) * (TM // halo), n_halo_blocks - 1), 0)),
            pl.BlockSpec((None, D_MODEL, half), lambda t: (layer, 0, mm(t)[1])),
            pl.BlockSpec((None, D_MODEL, half), lambda t: (layer, 0, nj + mm(t)[1])),
            pl.BlockSpec((None, 3, half), lambda t: (layer, 0, ep(t)[1])),
            pl.BlockSpec((None, 3, half), lambda t: (layer, 0, nj + ep(t)[1])),
            pl.BlockSpec((None, 1, half), lambda t: (layer, 0, ep(t)[1])),
            pl.BlockSpec((None, 1, half), lambda t: (layer, 0, nj + ep(t)[1])),
        ],
        out_specs=pl.BlockSpec((TM, half), lambda t: ep(t)),
        out_shape=jax.ShapeDtypeStruct((n_rows, D_FF), BF16),
        scratch_shapes=[
            pltpu.VMEM((ext, D_MODEL), BF16),
            pltpu.VMEM((ext, 2 * half), F32),
            pltpu.VMEM((ext, 2 * half), F32),
        ],
        compiler_params=_params(("arbitrary",), vmem),
        name="ffn_in_conv_glu",
    )(h, h, h, w_in, w_in, conv_w, conv_w, conv_b, conv_b)


def _rope_tables():
    rows = SEQ // GRID_W
    row = jnp.repeat(jnp.arange(rows), GRID_W).astype(F32)
    col = jnp.tile(jnp.arange(GRID_W), rows).astype(F32)
    inv = 1.0 / (ROPE_BASE ** (jnp.arange(0, ROPE_HALF, 2, dtype=F32) / ROPE_HALF))

    def tab(pos):
        a = pos[:, None] * inv[None, :]
        a = jnp.concatenate([a, a], axis=-1)
        return jnp.cos(a), jnp.sin(a)

    cr, sr = tab(row)
    cc, sc = tab(col)
    cos = jnp.concatenate([cr, cc], axis=-1)
    sin = jnp.concatenate([sr, sc], axis=-1)
    lane = jnp.arange(HEAD_DIM)
    sin = jnp.where((lane % ROPE_HALF) < ROPE_HALF // 2, -sin, sin)
    cos_all = jnp.concatenate([jnp.tile(cos, (BATCH, 1)), jnp.ones((N_CTX, HEAD_DIM), F32)], axis=0)
    sin_all = jnp.concatenate([jnp.tile(sin, (BATCH, 1)), jnp.zeros((N_CTX, HEAD_DIM), F32)], axis=0)
    return cos_all, sin_all


def _ctx_needed_after(i):
    return any((j % N_MIXERS) == 0 for j in range(i + 1, DEPTH))


def kernel(x, c, ctx, c_ctx, ada_w, ada_b, norm_g, attn_qkv, attn_o, attn_lambda, attn_subln_g,
           pool_w, pool_scale, ffn_w_in, ffn_conv_w, ffn_conv_b, ffn_w_out, final_g):
    assert x.shape == (BATCH, SEQ, D_MODEL) and ctx.shape == (BATCH, CTX_LEN, D_MODEL)
    xs = jnp.concatenate([x.reshape(N_LAT, D_MODEL), ctx.reshape(N_CTX, D_MODEL)], axis=0)
    cond = jnp.concatenate(
        [c, c_ctx[None, :], jnp.zeros((COND_ROWS - BATCH - 1, D_MODEL), F32)], axis=0)
    mod = _ada(cond, ada_w, ada_b)
    modr = mod.reshape(DEPTH, COND_ROWS, 6, D_MODEL).transpose(0, 2, 1, 3).reshape(
        DEPTH * 6 * COND_ROWS, 1, D_MODEL)
    cos_tab, sin_tab = _rope_tables()
    ones_row = jnp.ones((1, D_MODEL), F32)

    w_qkv = attn_qkv.astype(BF16)
    w_o = attn_o.astype(BF16)
    w_pool = pool_w.astype(BF16)
    w_in = ffn_w_in.astype(BF16)
    conv_b = ffn_conv_b.reshape(DEPTH, 1, 2 * D_FF)
    w_out = ffn_w_out.astype(BF16)

    for i in range(DEPTH):
        need_ctx = _ctx_needed_after(i)
        n_rows = N_ALL if need_ctx else N_LAT
        if i % N_MIXERS == 0:
            a = i // N_MIXERS
            lam_init = 0.8 - 0.6 * math.exp(-0.3 * i)
            h = _norm_mod_call(xs, norm_g[i, 0], modr, i, 0, N_ALL)
            qkv = _qkv_call(h, w_qkv, a, cos_tab, sin_tab, N_ALL)
            o_lat = _attn_call(qkv, attn_lambda[a], attn_subln_g[a], lam_init, latent=True)
            xs = _mm_res_call(o_lat, w_o, a, xs, modr, ones_row, i, 2, 0, tm=TM, tn=512)
            if need_ctx:
                o_ctx = _attn_call(qkv, attn_lambda[a], attn_subln_g[a], lam_init, latent=False)
                xs = _mm_res_call(o_ctx, w_o, a, xs, modr, ones_row, i, 2, N_LAT, tm=TM, tn=512)
        else:
            p = i // N_MIXERS
            y = _pool_call(xs, norm_g[i, 0], modr, i, n_rows)
            xs = _mm_res_call(y, w_pool, p, xs, modr, pool_scale[p].reshape(1, D_MODEL), i, 2, 0,
                              tm=TM, tn=POOL_GROUP, grouped=True)
        h = _norm_mod_call(xs, norm_g[i, 1], modr, i, 3, n_rows)
        act = _ffn_in_call(h, w_in, ffn_conv_w, conv_b, i, n_rows)
        xs = _mm_res_call(act, w_out, i, xs, modr, ones_row, i, 5, 0, tm=512, tn=512)
    return _final_norm(xs, final_g).reshape(BATCH, SEQ, D_MODEL)
```

```python
import functools
import math

import jax
import jax.numpy as jnp
from jax import lax
from jax.experimental import pallas as pl
from jax.experimental.pallas import tpu as pltpu

D_MODEL = 4096
BATCH = 4
SEQ = 4096
DEPTH = 4
GRID_W = 64
CTX_LEN = 256
N_MIXERS = 2
HEAD_DIM = 128
V_HEAD_DIM = 2 * HEAD_DIM
N_HEADS = D_MODEL // V_HEAD_DIM
ROPE_HALF = HEAD_DIM // 2
ROPE_BASE = 10000.0
POOL_WINDOWS = (2, 4, 8, 16)
N_POOL_GROUPS = len(POOL_WINDOWS)
POOL_GROUP = D_MODEL // N_POOL_GROUPS
D_FF = 7168
EPS = 1e-6
SUBLN_EPS = 1e-5

N_LAT = BATCH * SEQ
N_CTX = BATCH * CTX_LEN
N_ALL = N_LAT + N_CTX
COND_ROWS = 8

V7X_VMEM_BYTES = 64 * 1024 * 1024
VMEM_CAP_BYTES = V7X_VMEM_BYTES - 6 * 1024 * 1024
LANES = 128
BF16_SUBLANES = 16
LOG2_E = math.log2(math.e)

TM = 1024
BF16 = jnp.bfloat16
F32 = jnp.float32


def _params(semantics, vmem_bytes, flags=None):
    return pltpu.CompilerParams(
        dimension_semantics=semantics,
        vmem_limit_bytes=int(min(VMEM_CAP_BYTES, vmem_bytes)),
        flags=flags,
    )


def _mod_row(layer, chunk, tile_rows):
    n_lat_tiles = N_LAT // tile_rows
    tiles_per_seq = SEQ // tile_rows
    base = (layer * 6 + chunk) * COND_ROWS

    def idx(i):
        return base + jnp.where(i < n_lat_tiles, i // tiles_per_seq, BATCH)

    return idx


def _ada_kernel(cond_ref, w_ref, b_ref, o_ref):
    k = pl.program_id(2)
    c = cond_ref[...]
    c = c * (1.0 / (1.0 + jnp.exp(-c)))
    acc = jnp.dot(c.astype(BF16), w_ref[...].astype(BF16), preferred_element_type=F32)

    @pl.when(k == 0)
    def _():
        o_ref[...] = acc + b_ref[...]

    @pl.when(k != 0)
    def _():
        o_ref[...] += acc


def _ada(cond, ada_w, ada_b):
    tn, tk = 2048, 1024
    n = 6 * D_MODEL
    return pl.pallas_call(
        _ada_kernel,
        grid=(DEPTH, n // tn, D_MODEL // tk),
        in_specs=[
            pl.BlockSpec((COND_ROWS, tk), lambda i, j, k: (0, k)),
            pl.BlockSpec((None, tk, tn), lambda i, j, k: (i, k, j)),
            pl.BlockSpec((None, 1, tn), lambda i, j, k: (i, 0, j)),
        ],
        out_specs=pl.BlockSpec((None, COND_ROWS, tn), lambda i, j, k: (i, 0, j)),
        out_shape=jax.ShapeDtypeStruct((DEPTH, COND_ROWS, n), F32),
        compiler_params=_params(("arbitrary", "arbitrary", "arbitrary"), 3 * tk * tn * 4 + (8 << 20)),
        name="ada_mod",
    )(cond, ada_w, ada_b.reshape(DEPTH, 1, n))


def _norm_mod(x, g, sh, sc):
    ms = jnp.mean(x * x, axis=-1, keepdims=True)
    y = x * lax.rsqrt(ms + EPS) * g
    return y * (1.0 + sc) + sh


def _norm_mod_kernel(x_ref, g_ref, sh_ref, sc_ref, o_ref):
    o_ref[...] = _norm_mod(x_ref[...], g_ref[...], sh_ref[...], sc_ref[...]).astype(o_ref.dtype)


def _norm_mod_call(xs, g, modr, layer, chunk0, n_rows):
    r = 512
    sh = _mod_row(layer, chunk0, r)
    sc = _mod_row(layer, chunk0 + 1, r)
    return pl.pallas_call(
        _norm_mod_kernel,
        grid=(n_rows // r,),
        in_specs=[
            pl.BlockSpec((r, D_MODEL), lambda i: (i, 0)),
            pl.BlockSpec((1, D_MODEL), lambda i: (0, 0)),
            pl.BlockSpec((None, 1, D_MODEL), lambda i: (sh(i), 0, 0)),
            pl.BlockSpec((None, 1, D_MODEL), lambda i: (sc(i), 0, 0)),
        ],
        out_specs=pl.BlockSpec((r, D_MODEL), lambda i: (i, 0)),
        out_shape=jax.ShapeDtypeStruct((n_rows, D_MODEL), BF16),
        compiler_params=_params(("arbitrary",), 6 * r * D_MODEL * 4),
        name="norm_mod",
    )(xs, g.reshape(1, D_MODEL), modr, modr)


def _final_norm_kernel(x_ref, g_ref, o_ref):
    x = x_ref[...]
    ms = jnp.mean(x * x, axis=-1, keepdims=True)
    o_ref[...] = x * lax.rsqrt(ms + EPS) * g_ref[...]


def _final_norm(xs, g):
    r = 512
    return pl.pallas_call(
        _final_norm_kernel,
        grid=(N_LAT // r,),
        in_specs=[
            pl.BlockSpec((r, D_MODEL), lambda i: (i, 0)),
            pl.BlockSpec((1, D_MODEL), lambda i: (0, 0)),
        ],
        out_specs=pl.BlockSpec((r, D_MODEL), lambda i: (i, 0)),
        out_shape=jax.ShapeDtypeStruct((N_LAT, D_MODEL), F32),
        compiler_params=_params(("arbitrary",), 6 * r * D_MODEL * 4),
        name="final_norm",
    )(xs, g.reshape(1, D_MODEL))


def _qkv_kernel(a_ref, w_ref, cos_ref, sin_ref, o_ref, u0_ref, u1_ref, *, n_col_tiles, dot_chunks, rc,
                n_rope_blocks, n_q_blocks, q_scale):
    t = pl.program_id(0)
    tm, tn = o_ref.shape
    je = jnp.maximum(t - 1, 0) % n_col_tiles
    use_rope = je < n_rope_blocks
    qs = jnp.where(je < n_q_blocks, q_scale, 1.0).astype(F32)
    lane = lax.broadcasted_iota(jnp.int32, (rc, HEAD_DIM), 1)
    first_half = (lane % ROPE_HALF) < (ROPE_HALF // 2)

    @pl.when(t == 0)
    def _():
        u1_ref[...] = jnp.zeros(u1_ref.shape, F32)

    def work(uw_ref, ur_ref):
        per_chunk = (tm // rc) // len(dot_chunks)
        row = 0
        for c, size in enumerate(dot_chunks):
            for ri in range(c * per_chunk, (c + 1) * per_chunk):
                rows = pl.ds(ri * rc, rc)
                cos = jnp.where(use_rope, cos_ref[rows, :], 1.0)
                sin = jnp.where(use_rope, sin_ref[rows, :], 0.0)
                for ci in range(tn // HEAD_DIM):
                    cols = slice(ci * HEAD_DIM, (ci + 1) * HEAD_DIM)
                    x = ur_ref[rows, cols]
                    rot = jnp.where(first_half, pltpu.roll(x, HEAD_DIM - ROPE_HALF // 2, 1),
                                    pltpu.roll(x, ROPE_HALF // 2, 1))
                    o_ref[rows, cols] = ((x * cos + rot * sin) * qs).astype(o_ref.dtype)
            rows = pl.ds(row, size)
            uw_ref[rows, :] = jnp.dot(a_ref[rows, :], w_ref[...], preferred_element_type=F32)
            row += size

    @pl.when(t % 2 == 0)
    def _():
        work(u0_ref, u1_ref)

    @pl.when(t % 2 == 1)
    def _():
        work(u1_ref, u0_ref)


def _qkv_call(h, w, layer, cos_tab, sin_tab, n_rows):
    tn = 1024
    n = 3 * D_MODEL
    nj = n // tn
    n_steps = (n_rows // TM) * nj
    kern = functools.partial(
        _qkv_kernel, n_col_tiles=nj, dot_chunks=(256, 256, 256, 256), rc=64,
        n_rope_blocks=2 * D_MODEL // tn, n_q_blocks=D_MODEL // tn, q_scale=HEAD_DIM ** -0.5 * LOG2_E)
    vmem = (2 * (TM * D_MODEL * 2 + D_MODEL * tn * 2 + TM * tn * 2 + 2 * TM * HEAD_DIM * 4)
            + 2 * TM * tn * 4 + (12 << 20))

    def mm(t):
        tt = jnp.minimum(t, n_steps - 1)
        return tt // nj, tt % nj

    def ep(t):
        te = jnp.maximum(t - 1, 0)
        return te // nj, te % nj

    return pl.pallas_call(
        kern,
        grid=(n_steps + 1,),
        in_specs=[
            pl.BlockSpec((TM, D_MODEL), lambda t: (mm(t)[0], 0)),
            pl.BlockSpec((None, D_MODEL, tn), lambda t: (layer, 0, mm(t)[1])),
            pl.BlockSpec((TM, HEAD_DIM), lambda t: (ep(t)[0], 0)),
            pl.BlockSpec((TM, HEAD_DIM), lambda t: (ep(t)[0], 0)),
        ],
        out_specs=pl.BlockSpec((TM, tn), lambda t: ep(t)),
        out_shape=jax.ShapeDtypeStruct((n_rows, n), BF16),
        scratch_shapes=[pltpu.VMEM((TM, tn), F32), pltpu.VMEM((TM, tn), F32)],
        compiler_params=_params(("arbitrary",), vmem),
        name="qkv_rope",
    )(h, w, cos_tab, sin_tab)


def _attn_kernel(lam_ref, g_ref, q_ref, *refs, tq, tk, n_ctx_keys, n_lat_keys, lam_init):
    if n_lat_keys:
        kc_ref, vc_ref, kl_ref, vl_ref, o_ref, m_ref, l_ref, acc_ref = refs
    else:
        kc_ref, vc_ref, o_ref, m_ref, l_ref, acc_ref = refs
    m_ref[...] = jnp.full(m_ref.shape, -jnp.inf, F32)
    l_ref[...] = jnp.zeros(l_ref.shape, F32)
    acc_ref[...] = jnp.zeros(acc_ref.shape, F32)

    def scores(k):
        out = []
        for m in range(2):
            cols = slice(m * HEAD_DIM, (m + 1) * HEAD_DIM)
            out.append(lax.dot_general(q_ref[:, cols], k[:, cols], (((1,), (1,)), ((), ())),
                                       preferred_element_type=F32))
        return out

    def softmax_pv(s_maps, v):
        for m, s in enumerate(s_maps):
            n = s.shape[1]
            m_prev = m_ref[m]
            m_next = jnp.maximum(m_prev, jnp.max(s, axis=1, keepdims=True))
            alpha = jnp.exp2(m_prev - m_next)
            p = jnp.exp2(s - jnp.concatenate([m_next] * (n // LANES), axis=1))
            l_ref[m] = alpha * l_ref[m] + jnp.sum(p, axis=1, keepdims=True)
            m_ref[m] = m_next
            pv = jnp.dot(p.astype(BF16), v, preferred_element_type=F32)
            acc_ref[m] = acc_ref[m] * jnp.concatenate([alpha] * (V_HEAD_DIM // LANES), axis=1) + pv

    chunks = [(kc_ref, vc_ref, pl.ds(0, n_ctx_keys))]
    chunks += [(kl_ref, vl_ref, pl.ds(c * tk, tk)) for c in range(n_lat_keys // tk)]
    s_next = scores(chunks[0][0][chunks[0][2], :])
    for c, (_, v_ref, rows) in enumerate(chunks):
        s_cur = s_next
        if c + 1 < len(chunks):
            k_next, _, rows_next = chunks[c + 1]
            s_next = scores(k_next[rows_next, :])
        softmax_pv(s_cur, v_ref[rows, :])

    lp = lam_ref[...]
    lam = (jnp.exp(jnp.sum(lp[0:1] * lp[1:2], axis=1, keepdims=True))
           - jnp.exp(jnp.sum(lp[2:3] * lp[3:4], axis=1, keepdims=True)) + lam_init)
    rep = V_HEAD_DIM // LANES
    o1 = acc_ref[0] / jnp.concatenate([l_ref[0]] * rep, axis=1)
    o2 = acc_ref[1] / jnp.concatenate([l_ref[1]] * rep, axis=1)
    o = o1 - lam * o2
    ms = jnp.mean(o * o, axis=-1, keepdims=True)
    y = o * lax.rsqrt(ms + SUBLN_EPS) * g_ref[...]
    o_ref[...] = (y * (1.0 - lam_init)).astype(o_ref.dtype)


def _attn_call(qkv, lam_p, subln_g, lam_init, latent):
    hq = D_MODEL // V_HEAD_DIM
    ctx_row0 = N_LAT // CTX_LEN
    if latent:
        tq, tk, n_lat_keys, n_out = 512, 1024, SEQ, N_LAT
        n_q = SEQ // tq
        q_map = lambda b, h, qi: (b * (SEQ // tq) + qi, h)
    else:
        tq, tk, n_q, n_lat_keys, n_out = CTX_LEN, 512, 1, 0, N_CTX
        q_map = lambda b, h, qi: (ctx_row0 + b, h)
    in_specs = [
        pl.BlockSpec((4, HEAD_DIM), lambda b, h, qi: (0, 0)),
        pl.BlockSpec((1, V_HEAD_DIM), lambda b, h, qi: (0, 0)),
        pl.BlockSpec((tq, V_HEAD_DIM), q_map),
        pl.BlockSpec((CTX_LEN, V_HEAD_DIM), lambda b, h, qi: (ctx_row0 + b, hq + h)),
        pl.BlockSpec((CTX_LEN, V_HEAD_DIM), lambda b, h, qi: (ctx_row0 + b, 2 * hq + h)),
    ]
    args = [lam_p, subln_g.reshape(1, V_HEAD_DIM), qkv, qkv, qkv]
    if latent:
        in_specs += [
            pl.BlockSpec((SEQ, V_HEAD_DIM), lambda b, h, qi: (b, hq + h)),
            pl.BlockSpec((SEQ, V_HEAD_DIM), lambda b, h, qi: (b, 2 * hq + h)),
        ]
        args += [qkv, qkv]
        out_map = lambda b, h, qi: (b * (SEQ // tq) + qi, h)
    else:
        out_map = lambda b, h, qi: (b, h)
    kern = functools.partial(_attn_kernel, tq=tq, tk=tk, n_ctx_keys=CTX_LEN, n_lat_keys=n_lat_keys,
                             lam_init=lam_init)
    return pl.pallas_call(
        kern,
        grid=(BATCH, N_HEADS, n_q),
        in_specs=in_specs,
        out_specs=pl.BlockSpec((tq, V_HEAD_DIM), out_map),
        out_shape=jax.ShapeDtypeStruct((n_out, D_MODEL), BF16),
        scratch_shapes=[
            pltpu.VMEM((2, tq, LANES), F32),
            pltpu.VMEM((2, tq, LANES), F32),
            pltpu.VMEM((2, tq, V_HEAD_DIM), F32),
        ],
        compiler_params=_params(("arbitrary", "arbitrary", "arbitrary"), 40 << 20),
        name="diff_attn_lat" if latent else "diff_attn_ctx",
    )(*args)


def _mm_res_kernel(a_ref, w_ref, x_ref, gate_ref, cs_ref, o_ref, *, chunk):
    tm = o_ref.shape[0]
    for c in range(tm // chunk):
        rows = pl.ds(c * chunk, chunk)
        acc = jnp.dot(a_ref[rows, :], w_ref[...], preferred_element_type=F32)
        o_ref[rows, :] = x_ref[rows, :] + gate_ref[...] * (acc * cs_ref[...])


def _mm_res_call(a, w, w_idx, xs, modr, col_scale, layer, gate_chunk, row0, *, tm, tn, grouped=False):
    k = w.shape[-2]
    assert a.shape[0] % tm == 0 and row0 % tm == 0
    n_row_tiles = a.shape[0] // tm
    row_tile0 = row0 // tm
    gate = _mod_row(layer, gate_chunk, tm)
    if grouped:
        assert tn == w.shape[-1]
        a_spec = pl.BlockSpec((tm, k), lambda i, j: (i, j))
        w_spec = pl.BlockSpec((None, None, k, tn), lambda i, j: (w_idx, j, 0, 0))
    else:
        a_spec = pl.BlockSpec((tm, k), lambda i, j: (i, 0))
        w_spec = pl.BlockSpec((None, k, tn), lambda i, j: (w_idx, 0, j))
    x_spec = pl.BlockSpec((tm, tn), lambda i, j: (row_tile0 + i, j))
    vmem = 2 * (tm * k * 2 + k * tn * 2 + 2 * tm * tn * 4) + (12 << 20)
    return pl.pallas_call(
        functools.partial(_mm_res_kernel, chunk=min(tm, 512)),
        grid=(n_row_tiles, D_MODEL // tn),
        in_specs=[
            a_spec,
            w_spec,
            x_spec,
            pl.BlockSpec((None, 1, tn), lambda i, j: (gate(row_tile0 + i), 0, j)),
            pl.BlockSpec((1, tn), lambda i, j: (0, j)),
        ],
        out_specs=x_spec,
        out_shape=jax.ShapeDtypeStruct(xs.shape, F32),
        input_output_aliases={2: 0},
        compiler_params=_params(("arbitrary", "arbitrary"), vmem),
        name="mm_residual",
    )(a, w, xs, modr, col_scale)


def _pool_kernel(x_ref, xp_ref, xn_ref, g_ref, sh_ref, sc_ref, o_ref, hs_ref, *, r, halo, n_lat_tiles):
    i = pl.program_id(0)
    seq_len = jnp.where(i < n_lat_tiles, SEQ, CTX_LEN)
    pos0 = (i * r) & (seq_len - 1)
    is_first = pos0 == 0
    is_last = pos0 + r == seq_len
    g, sh, sc = g_ref[...], sh_ref[...], sc_ref[...]
    hs_ref[0:halo, :] = jnp.where(is_first, 0.0, _norm_mod(xp_ref[...], g, sh, sc))
    hs_ref[halo:halo + r, :] = _norm_mod(x_ref[...], g, sh, sc)
    hs_ref[halo + r:, :] = jnp.where(is_last, 0.0, _norm_mod(xn_ref[...], g, sh, sc))

    rc, cc = 128, 256
    for grp, w in enumerate(POOL_WINDOWS):
        before, after = w // 2, w - 1 - w // 2
        for ri in range(r // rc):
            pos = pos0 + ri * rc + lax.broadcasted_iota(jnp.int32, (rc, 1), 0)
            lo = jnp.maximum(pos - before, 0)
            hi = jnp.minimum(pos + after, seq_len - 1)
            inv_cnt = 1.0 / (hi - lo + 1).astype(F32)
            for ci in range(POOL_GROUP // cc):
                cols = slice(grp * POOL_GROUP + ci * cc, grp * POOL_GROUP + (ci + 1) * cc)
                acc = hs_ref[pl.ds(halo + ri * rc - before, rc), cols]
                for s in range(-before + 1, after + 1):
                    acc = acc + hs_ref[pl.ds(halo + ri * rc + s, rc), cols]
                y = acc * inv_cnt - hs_ref[pl.ds(halo + ri * rc, rc), cols]
                o_ref[pl.ds(ri * rc, rc), cols] = y.astype(o_ref.dtype)


def _pool_call(xs, g, modr, layer, n_rows):
    r, halo = 256, 8
    n_halo_blocks = n_rows // halo
    sh = _mod_row(layer, 0, r)
    sc = _mod_row(layer, 1, r)
    kern = functools.partial(_pool_kernel, r=r, halo=halo, n_lat_tiles=N_LAT // r)
    return pl.pallas_call(
        kern,
        grid=(n_rows // r,),
        in_specs=[
            pl.BlockSpec((r, D_MODEL), lambda i: (i, 0)),
            pl.BlockSpec((halo, D_MODEL), lambda i: (jnp.maximum(i * (r // halo) - 1, 0), 0)),
            pl.BlockSpec((halo, D_MODEL), lambda i: (jnp.minimum((i + 1) * (r // halo), n_halo_blocks - 1), 0)),
            pl.BlockSpec((1, D_MODEL), lambda i: (0, 0)),
            pl.BlockSpec((None, 1, D_MODEL), lambda i: (sh(i), 0, 0)),
            pl.BlockSpec((None, 1, D_MODEL), lambda i: (sc(i), 0, 0)),
        ],
        out_specs=pl.BlockSpec((r, D_MODEL), lambda i: (i, 0)),
        out_shape=jax.ShapeDtypeStruct((n_rows, D_MODEL), BF16),
        scratch_shapes=[pltpu.VMEM((r + 2 * halo, D_MODEL), F32)],
        compiler_params=_params(("arbitrary",), 32 << 20),
        name="norm_mod_pool",
    )(xs, xs, xs, g.reshape(1, D_MODEL), modr, modr)


def _ffn_in_kernel(a_ref, ap_ref, an_ref, wg_ref, wv_ref, cwg_ref, cwv_ref, cbg_ref, cbv_ref, o_ref,
                   aext_ref, u0_ref, u1_ref, *, halo, n_col_tiles, n_lat_tiles, dot_chunks, rc):
    t = pl.program_id(0)
    tm, half = o_ref.shape
    j = jnp.minimum(t, pl.num_programs(0) - 2) % n_col_tiles

    @pl.when(t == 0)
    def _():
        u1_ref[...] = jnp.zeros(u1_ref.shape, F32)

    @pl.when(j == 0)
    def _():
        aext_ref[0:halo, :] = ap_ref[...]
        aext_ref[halo:halo + tm, :] = a_ref[...]
        aext_ref[halo + tm:, :] = an_ref[...]

    ie = jnp.maximum(t - 1, 0) // n_col_tiles
    seq_len = jnp.where(ie < n_lat_tiles, SEQ, CTX_LEN)
    sub = lax.broadcasted_iota(jnp.int32, (8, 1), 0)

    def conv(ur_ref, cols, r0, cw_ref, cb_ref, ccols):
        up = ur_ref[pl.ds(halo + r0 - 1, rc), cols]
        un = ur_ref[pl.ds(halo + r0 + 1, rc), cols]
        uc = ur_ref[pl.ds(halo + r0, rc), cols]
        if r0 % CTX_LEN == 0:
            pos = (ie * tm + r0 + sub) & (seq_len - 1)
            up = jnp.concatenate([jnp.where(pos != 0, up[0:8], 0.0), up[8:]], axis=0)
        if (r0 + rc) % CTX_LEN == 0:
            pos = (ie * tm + r0 + rc - 8 + sub) & (seq_len - 1)
            un = jnp.concatenate([un[:rc - 8], jnp.where(pos != seq_len - 1, un[rc - 8:], 0.0)], axis=0)
        return (up * cw_ref[0:1, ccols] + uc * cw_ref[1:2, ccols] + un * cw_ref[2:3, ccols]
                + cb_ref[:, ccols])

    def work(uw_ref, ur_ref):
        n_pieces = tm // rc
        per_chunk = n_pieces // len(dot_chunks)
        row = 0
        for c, size in enumerate(dot_chunks):
            for ri in range(c * per_chunk, (c + 1) * per_chunk):
                r0 = ri * rc
                for ci in range(half // LANES):
                    ccols = slice(ci * LANES, (ci + 1) * LANES)
                    gate = conv(ur_ref, ccols, r0, cwg_ref, cbg_ref, ccols)
                    val = conv(ur_ref, slice(half + ci * LANES, half + (ci + 1) * LANES), r0, cwv_ref,
                               cbv_ref, ccols)
                    o_ref[pl.ds(r0, rc), ccols] = (
                        gate * (1.0 / (1.0 + jnp.exp(-gate))) * val).astype(o_ref.dtype)
            rows = pl.ds(row, size)
            a = aext_ref[rows, :]
            uw_ref[rows, 0:half] = jnp.dot(a, wg_ref[...], preferred_element_type=F32)
            uw_ref[rows, half:] = jnp.dot(a, wv_ref[...], preferred_element_type=F32)
            row += size

    @pl.when(t % 2 == 0)
    def _():
        work(u0_ref, u1_ref)

    @pl.when(t % 2 == 1)
    def _():
        work(u1_ref, u0_ref)


def _ffn_in_call(h, w_in, conv_w, conv_b, layer, n_rows):
    half, halo = 512, BF16_SUBLANES
    n_halo_blocks = n_rows // halo
    ext = TM + 2 * halo
    nj = D_FF // half
    n_steps = (n_rows // TM) * nj
    kern = functools.partial(_ffn_in_kernel, halo=halo, n_col_tiles=nj, n_lat_tiles=N_LAT // TM,
                             dot_chunks=(272, 272, 256, 256), rc=32)
    vmem = (2 * (TM * D_MODEL * 2 + 2 * D_MODEL * half * 2 + TM * half * 2 + 2 * halo * D_MODEL * 2)
            + ext * D_MODEL * 2 + 2 * ext * 2 * half * 4 + (12 << 20))

    def mm(t):
        tt = jnp.minimum(t, n_steps - 1)
        return tt // nj, tt % nj

    def ep(t):
        te = jnp.maximum(t - 1, 0)
        return te // nj, te % nj

    return pl.pallas_call(
        kern,
        grid=(n_steps + 1,),
        in_specs=[
            pl.BlockSpec((TM, D_MODEL), lambda t: (mm(t)[0], 0)),
            pl.BlockSpec((halo, D_MODEL), lambda t: (jnp.maximum(mm(t)[0] * (TM // halo) - 1, 0), 0)),
            pl.BlockSpec((halo, D_MODEL),
                         lambda t: (jnp.minimum((mm(t)[0] + 1) * (TM // halo), n_halo_blocks - 1), 0)),
            pl.BlockSpec((None, D_MODEL, half), lambda t: (layer, 0, mm(t)[1])),
            pl.BlockSpec((None, D_MODEL, half), lambda t: (layer, 0, nj + mm(t)[1])),
            pl.BlockSpec((None, 3, half), lambda t: (layer, 0, ep(t)[1])),
            pl.BlockSpec((None, 3, half), lambda t: (layer, 0, nj + ep(t)[1])),
            pl.BlockSpec((None, 1, half), lambda t: (layer, 0, ep(t)[1])),
            pl.BlockSpec((None, 1, half), lambda t: (layer, 0, nj + ep(t)[1])),
        ],
        out_specs=pl.BlockSpec((TM, half), lambda t: ep(t)),
        out_shape=jax.ShapeDtypeStruct((n_rows, D_FF), BF16),
        scratch_shapes=[
            pltpu.VMEM((ext, D_MODEL), BF16),
            pltpu.VMEM((ext, 2 * half), F32),
            pltpu.VMEM((ext, 2 * half), F32),
        ],
        compiler_params=_params(("arbitrary",), vmem),
        name="ffn_in_conv_glu",
    )(h, h, h, w_in, w_in, conv_w, conv_w, conv_b, conv_b)


def _rope_tables():
    rows = SEQ // GRID_W
    row = jnp.repeat(jnp.arange(rows), GRID_W).astype(F32)
    col = jnp.tile(jnp.arange(GRID_W), rows).astype(F32)
    inv = 1.0 / (ROPE_BASE ** (jnp.arange(0, ROPE_HALF, 2, dtype=F32) / ROPE_HALF))

    def tab(pos):
        a = pos[:, None] * inv[None, :]
        a = jnp.concatenate([a, a], axis=-1)
        return jnp.cos(a), jnp.sin(a)

    cr, sr = tab(row)
    cc, sc = tab(col)
    cos = jnp.concatenate([cr, cc], axis=-1)
    sin = jnp.concatenate([sr, sc], axis=-1)
    lane = jnp.arange(HEAD_DIM)
    sin = jnp.where((lane % ROPE_HALF) < ROPE_HALF // 2, -sin, sin)
    cos_all = jnp.concatenate([jnp.tile(cos, (BATCH, 1)), jnp.ones((N_CTX, HEAD_DIM), F32)], axis=0)
    sin_all = jnp.concatenate([jnp.tile(sin, (BATCH, 1)), jnp.zeros((N_CTX, HEAD_DIM), F32)], axis=0)
    return cos_all, sin_all


def _ctx_needed_after(i):
    return any((j % N_MIXERS) == 0 for j in range(i + 1, DEPTH))


def kernel(x, c, ctx, c_ctx, ada_w, ada_b, norm_g, attn_qkv, attn_o, attn_lambda, attn_subln_g,
           pool_w, pool_scale, ffn_w_in, ffn_conv_w, ffn_conv_b, ffn_w_out, final_g):
    assert x.shape == (BATCH, SEQ, D_MODEL) and ctx.shape == (BATCH, CTX_LEN, D_MODEL)
    xs = jnp.concatenate([x.reshape(N_LAT, D_MODEL), ctx.reshape(N_CTX, D_MODEL)], axis=0)
    cond = jnp.concatenate(
        [c, c_ctx[None, :], jnp.zeros((COND_ROWS - BATCH - 1, D_MODEL), F32)], axis=0)
    mod = _ada(cond, ada_w, ada_b)
    modr = mod.reshape(DEPTH, COND_ROWS, 6, D_MODEL).transpose(0, 2, 1, 3).reshape(
        DEPTH * 6 * COND_ROWS, 1, D_MODEL)
    cos_tab, sin_tab = _rope_tables()
    ones_row = jnp.ones((1, D_MODEL), F32)

    w_qkv = attn_qkv.astype(BF16)
    w_o = attn_o.astype(BF16)
    w_pool = pool_w.astype(BF16)
    w_in = ffn_w_in.astype(BF16)
    conv_b = ffn_conv_b.reshape(DEPTH, 1, 2 * D_FF)
    w_out = ffn_w_out.astype(BF16)

    for i in range(DEPTH):
        need_ctx = _ctx_needed_after(i)
        n_rows = N_ALL if need_ctx else N_LAT
        if i % N_MIXERS == 0:
            a = i // N_MIXERS
            lam_init = 0.8 - 0.6 * math.exp(-0.3 * i)
            h = _norm_mod_call(xs, norm_g[i, 0], modr, i, 0, N_ALL)
            qkv = _qkv_call(h, w_qkv, a, cos_tab, sin_tab, N_ALL)
            o_lat = _attn_call(qkv, attn_lambda[a], attn_subln_g[a], lam_init, latent=True)
            xs = _mm_res_call(o_lat, w_o, a, xs, modr, ones_row, i, 2, 0, tm=TM, tn=512)
            if need_ctx:
                o_ctx = _attn_call(qkv, attn_lambda[a], attn_subln_g[a], lam_init, latent=False)
                xs = _mm_res_call(o_ctx, w_o, a, xs, modr, ones_row, i, 2, N_LAT, tm=TM, tn=512)
        else:
            p = i // N_MIXERS
            y = _pool_call(xs, norm_g[i, 0], modr, i, n_rows)
            xs = _mm_res_call(y, w_pool, p, xs, modr, pool_scale[p].reshape(1, D_MODEL), i, 2, 0,
                              tm=TM, tn=POOL_GROUP, grouped=True)
        h = _norm_mod_call(xs, norm_g[i, 1], modr, i, 3, n_rows)
        act = _ffn_in_call(h, w_in, ffn_conv_w, conv_b, i, n_rows)
        xs = _mm_res_call(act, w_out, i, xs, modr, ones_row, i, 5, 0, tm=512, tn=512)
    return _final_norm(xs, final_g).reshape(BATCH, SEQ, D_MODEL)
```

```python
import functools
import math

import jax
import jax.numpy as jnp
from jax import lax
from jax.experimental import pallas as pl
from jax.experimental.pallas import tpu as pltpu

D_MODEL = 4096
BATCH = 4
SEQ = 4096
DEPTH = 4
GRID_W = 64
CTX_LEN = 256
N_MIXERS = 2
HEAD_DIM = 128
V_HEAD_DIM = 2 * HEAD_DIM
N_HEADS = D_MODEL // V_HEAD_DIM
ROPE_HALF = HEAD_DIM // 2
ROPE_BASE = 10000.0
POOL_WINDOWS = (2, 4, 8, 16)
N_POOL_GROUPS = len(POOL_WINDOWS)
POOL_GROUP = D_MODEL // N_POOL_GROUPS
D_FF = 7168
EPS = 1e-6
SUBLN_EPS = 1e-5

N_LAT = BATCH * SEQ
N_CTX = BATCH * CTX_LEN
N_ALL = N_LAT + N_CTX
COND_ROWS = 8

V7X_VMEM_BYTES = 64 * 1024 * 1024
VMEM_CAP_BYTES = V7X_VMEM_BYTES - 6 * 1024 * 1024
LANES = 128
BF16_SUBLANES = 16
LOG2_E = math.log2(math.e)

TM = 1024
BF16 = jnp.bfloat16
F32 = jnp.float32


def _params(semantics, vmem_bytes, flags=None):
    return pltpu.CompilerParams(
        dimension_semantics=semantics,
        vmem_limit_bytes=int(min(VMEM_CAP_BYTES, vmem_bytes)),
        flags=flags,
    )


def _mod_row(layer, chunk, tile_rows):
    n_lat_tiles = N_LAT // tile_rows
    tiles_per_seq = SEQ // tile_rows
    base = (layer * 6 + chunk) * COND_ROWS

    def idx(i):
        return base + jnp.where(i < n_lat_tiles, i // tiles_per_seq, BATCH)

    return idx


def _ada_kernel(cond_ref, w_ref, b_ref, o_ref):
    k = pl.program_id(2)
    c = cond_ref[...]
    c = c * (1.0 / (1.0 + jnp.exp(-c)))
    acc = jnp.dot(c.astype(BF16), w_ref[...].astype(BF16), preferred_element_type=F32)

    @pl.when(k == 0)
    def _():
        o_ref[...] = acc + b_ref[...]

    @pl.when(k != 0)
    def _():
        o_ref[...] += acc


def _ada(cond, ada_w, ada_b):
    tn, tk = 2048, 1024
    n = 6 * D_MODEL
    return pl.pallas_call(
        _ada_kernel,
        grid=(DEPTH, n // tn, D_MODEL // tk),
        in_specs=[
            pl.BlockSpec((COND_ROWS, tk), lambda i, j, k: (0, k)),
            pl.BlockSpec((None, tk, tn), lambda i, j, k: (i, k, j)),
            pl.BlockSpec((None, 1, tn), lambda i, j, k: (i, 0, j)),
        ],
        out_specs=pl.BlockSpec((None, COND_ROWS, tn), lambda i, j, k: (i, 0, j)),
        out_shape=jax.ShapeDtypeStruct((DEPTH, COND_ROWS, n), F32),
        compiler_params=_params(("arbitrary", "arbitrary", "arbitrary"), 3 * tk * tn * 4 + (8 << 20)),
        name="ada_mod",
    )(cond, ada_w, ada_b.reshape(DEPTH, 1, n))


def _norm_mod(x, g, sh, sc):
    ms = jnp.mean(x * x, axis=-1, keepdims=True)
    y = x * lax.rsqrt(ms + EPS) * g
    return y * (1.0 + sc) + sh


def _norm_mod_kernel(x_ref, g_ref, sh_ref, sc_ref, o_ref):
    o_ref[...] = _norm_mod(x_ref[...], g_ref[...], sh_ref[...], sc_ref[...]).astype(o_ref.dtype)


def _norm_mod_call(xs, g, modr, layer, chunk0, n_rows):
    r = 512
    sh = _mod_row(layer, chunk0, r)
    sc = _mod_row(layer, chunk0 + 1, r)
    return pl.pallas_call(
        _norm_mod_kernel,
        grid=(n_rows // r,),
        in_specs=[
            pl.BlockSpec((r, D_MODEL), lambda i: (i, 0)),
            pl.BlockSpec((1, D_MODEL), lambda i: (0, 0)),
            pl.BlockSpec((None, 1, D_MODEL), lambda i: (sh(i), 0, 0)),
            pl.BlockSpec((None, 1, D_MODEL), lambda i: (sc(i), 0, 0)),
        ],
        out_specs=pl.BlockSpec((r, D_MODEL), lambda i: (i, 0)),
        out_shape=jax.ShapeDtypeStruct((n_rows, D_MODEL), BF16),
        compiler_params=_params(("arbitrary",), 6 * r * D_MODEL * 4),
        name="norm_mod",
    )(xs, g.reshape(1, D_MODEL), modr, modr)


def _final_norm_kernel(x_ref, g_ref, o_ref):
    x = x_ref[...]
    ms = jnp.mean(x * x, axis=-1, keepdims=True)
    o_ref[...] = x * lax.rsqrt(ms + EPS) * g_ref[...]


def _final_norm(xs, g):
    r = 512
    return pl.pallas_call(
        _final_norm_kernel,
        grid=(N_LAT // r,),
        in_specs=[
            pl.BlockSpec((r, D_MODEL), lambda i: (i, 0)),
            pl.BlockSpec((1, D_MODEL), lambda i: (0, 0)),
        ],
        out_specs=pl.BlockSpec((r, D_MODEL), lambda i: (i, 0)),
        out_shape=jax.ShapeDtypeStruct((N_LAT, D_MODEL), F32),
        compiler_params=_params(("arbitrary",), 6 * r * D_MODEL * 4),
        name="final_norm",
    )(xs, g.reshape(1, D_MODEL))


def _qkv_kernel(a_ref, w_ref, cos_ref, sin_ref, o_ref, u0_ref, u1_ref, *, n_col_tiles, dot_chunks, rc,
                n_rope_blocks, n_q_blocks, q_scale):
    t = pl.program_id(0)
    tm, tn = o_ref.shape
    je = jnp.maximum(t - 1, 0) % n_col_tiles
    use_rope = je < n_rope_blocks
    qs = jnp.where(je < n_q_blocks, q_scale, 1.0).astype(F32)
    lane = lax.broadcasted_iota(jnp.int32, (rc, HEAD_DIM), 1)
    first_half = (lane % ROPE_HALF) < (ROPE_HALF // 2)

    @pl.when(t == 0)
    def _():
        u1_ref[...] = jnp.zeros(u1_ref.shape, F32)

    def work(uw_ref, ur_ref):
        per_chunk = (tm // rc) // len(dot_chunks)
        row = 0
        for c, size in enumerate(dot_chunks):
            for ri in range(c * per_chunk, (c + 1) * per_chunk):
                rows = pl.ds(ri * rc, rc)
                cos = jnp.where(use_rope, cos_ref[rows, :], 1.0)
                sin = jnp.where(use_rope, sin_ref[rows, :], 0.0)
                for ci in range(tn // HEAD_DIM):
                    cols = slice(ci * HEAD_DIM, (ci + 1) * HEAD_DIM)
                    x = ur_ref[rows, cols]
                    rot = jnp.where(first_half, pltpu.roll(x, HEAD_DIM - ROPE_HALF // 2, 1),
                                    pltpu.roll(x, ROPE_HALF // 2, 1))
                    o_ref[rows, cols] = ((x * cos + rot * sin) * qs).astype(o_ref.dtype)
            rows = pl.ds(row, size)
            uw_ref[rows, :] = jnp.dot(a_ref[rows, :], w_ref[...], preferred_element_type=F32)
            row += size

    @pl.when(t % 2 == 0)
    def _():
        work(u0_ref, u1_ref)

    @pl.when(t % 2 == 1)
    def _():
        work(u1_ref, u0_ref)


def _qkv_call(h, w, layer, cos_tab, sin_tab, n_rows):
    tn = 1024
    n = 3 * D_MODEL
    nj = n // tn
    n_steps = (n_rows // TM) * nj
    kern = functools.partial(
        _qkv_kernel, n_col_tiles=nj, dot_chunks=(256, 256, 256, 256), rc=64,
        n_rope_blocks=2 * D_MODEL // tn, n_q_blocks=D_MODEL // tn, q_scale=HEAD_DIM ** -0.5 * LOG2_E)
    vmem = (2 * (TM * D_MODEL * 2 + D_MODEL * tn * 2 + TM * tn * 2 + 2 * TM * HEAD_DIM * 4)
            + 2 * TM * tn * 4 + (12 << 20))

    def mm(t):
        tt = jnp.minimum(t, n_steps - 1)
        return tt // nj, tt % nj

    def ep(t):
        te = jnp.maximum(t - 1, 0)
        return te // nj, te % nj

    return pl.pallas_call(
        kern,
        grid=(n_steps + 1,),
        in_specs=[
            pl.BlockSpec((TM, D_MODEL), lambda t: (mm(t)[0], 0)),
            pl.BlockSpec((None, D_MODEL, tn), lambda t: (layer, 0, mm(t)[1])),
            pl.BlockSpec((TM, HEAD_DIM), lambda t: (ep(t)[0], 0)),
            pl.BlockSpec((TM, HEAD_DIM), lambda t: (ep(t)[0], 0)),
        ],
        out_specs=pl.BlockSpec((TM, tn), lambda t: ep(t)),
        out_shape=jax.ShapeDtypeStruct((n_rows, n), BF16),
        scratch_shapes=[pltpu.VMEM((TM, tn), F32), pltpu.VMEM((TM, tn), F32)],
        compiler_params=_params(("arbitrary",), vmem),
        name="qkv_rope",
    )(h, w, cos_tab, sin_tab)


def _attn_kernel(lam_ref, g_ref, q_ref, *refs, tq, tk, n_ctx_keys, n_lat_keys, lam_init):
    if n_lat_keys:
        kc_ref, vc_ref, kl_ref, vl_ref, o_ref, m_ref, l_ref, acc_ref = refs
    else:
        kc_ref, vc_ref, o_ref, m_ref, l_ref, acc_ref = refs
    m_ref[...] = jnp.full(m_ref.shape, -jnp.inf, F32)
    l_ref[...] = jnp.zeros(l_ref.shape, F32)
    acc_ref[...] = jnp.zeros(acc_ref.shape, F32)

    def scores(k):
        out = []
        for m in range(2):
            cols = slice(m * HEAD_DIM, (m + 1) * HEAD_DIM)
            out.append(lax.dot_general(q_ref[:, cols], k[:, cols], (((1,), (1,)), ((), ())),
                                       preferred_element_type=F32))
        return out

    def softmax_pv(s_maps, v):
        for m, s in enumerate(s_maps):
            n = s.shape[1]
            m_prev = m_ref[m]
            m_next = jnp.maximum(m_prev, jnp.max(s, axis=1, keepdims=True))
            alpha = jnp.exp2(m_prev - m_next)
            p = jnp.exp2(s - jnp.concatenate([m_next] * (n // LANES), axis=1))
            l_ref[m] = alpha * l_ref[m] + jnp.sum(p, axis=1, keepdims=True)
            m_ref[m] = m_next
            pv = jnp.dot(p.astype(BF16), v, preferred_element_type=F32)
            acc_ref[m] = acc_ref[m] * jnp.concatenate([alpha] * (V_HEAD_DIM // LANES), axis=1) + pv

    chunks = [(kc_ref, vc_ref, pl.ds(0, n_ctx_keys))]
    chunks += [(kl_ref, vl_ref, pl.ds(c * tk, tk)) for c in range(n_lat_keys // tk)]
    s_next = scores(chunks[0][0][chunks[0][2], :])
    for c, (_, v_ref, rows) in enumerate(chunks):
        s_cur = s_next
        if c + 1 < len(chunks):
            k_next, _, rows_next = chunks[c + 1]
            s_next = scores(k_next[rows_next, :])
        softmax_pv(s_cur, v_ref[rows, :])

    lp = lam_ref[...]
    lam = (jnp.exp(jnp.sum(lp[0:1] * lp[1:2], axis=1, keepdims=True))
           - jnp.exp(jnp.sum(lp[2:3] * lp[3:4], axis=1, keepdims=True)) + lam_init)
    rep = V_HEAD_DIM // LANES
    o1 = acc_ref[0] / jnp.concatenate([l_ref[0]] * rep, axis=1)
    o2 = acc_ref[1] / jnp.concatenate([l_ref[1]] * rep, axis=1)
    o = o1 - lam * o2
    ms = jnp.mean(o * o, axis=-1, keepdims=True)
    y = o * lax.rsqrt(ms + SUBLN_EPS) * g_ref[...]
    o_ref[...] = (y * (1.0 - lam_init)).astype(o_ref.dtype)


def _attn_call(qkv, lam_p, subln_g, lam_init, latent):
    hq = D_MODEL // V_HEAD_DIM
    ctx_row0 = N_LAT // CTX_LEN
    if latent:
        tq, tk, n_lat_keys, n_out = 512, 1024, SEQ, N_LAT
        n_q = SEQ // tq
        q_map = lambda b, h, qi: (b * (SEQ // tq) + qi, h)
    else:
        tq, tk, n_q, n_lat_keys, n_out = CTX_LEN, 512, 1, 0, N_CTX
        q_map = lambda b, h, qi: (ctx_row0 + b, h)
    in_specs = [
        pl.BlockSpec((4, HEAD_DIM), lambda b, h, qi: (0, 0)),
        pl.BlockSpec((1, V_HEAD_DIM), lambda b, h, qi: (0, 0)),
        pl.BlockSpec((tq, V_HEAD_DIM), q_map),
        pl.BlockSpec((CTX_LEN, V_HEAD_DIM), lambda b, h, qi: (ctx_row0 + b, hq + h)),
        pl.BlockSpec((CTX_LEN, V_HEAD_DIM), lambda b, h, qi: (ctx_row0 + b, 2 * hq + h)),
    ]
    args = [lam_p, subln_g.reshape(1, V_HEAD_DIM), qkv, qkv, qkv]
    if latent:
        in_specs += [
            pl.BlockSpec((SEQ, V_HEAD_DIM), lambda b, h, qi: (b, hq + h)),
            pl.BlockSpec((SEQ, V_HEAD_DIM), lambda b, h, qi: (b, 2 * hq + h)),
        ]
        args += [qkv, qkv]
        out_map = lambda b, h, qi: (b * (SEQ // tq) + qi, h)
    else:
        out_map = lambda b, h, qi: (b, h)
    kern = functools.partial(_attn_kernel, tq=tq, tk=tk, n_ctx_keys=CTX_LEN, n_lat_keys=n_lat_keys,
                             lam_init=lam_init)
    return pl.pallas_call(
        kern,
        grid=(BATCH, N_HEADS, n_q),
        in_specs=in_specs,
        out_specs=pl.BlockSpec((tq, V_HEAD_DIM), out_map),
        out_shape=jax.ShapeDtypeStruct((n_out, D_MODEL), BF16),
        scratch_shapes=[
            pltpu.VMEM((2, tq, LANES), F32),
            pltpu.VMEM((2, tq, LANES), F32),
            pltpu.VMEM((2, tq, V_HEAD_DIM), F32),
        ],
        compiler_params=_params(("arbitrary", "arbitrary", "arbitrary"), 40 << 20),
        name="diff_attn_lat" if latent else "diff_attn_ctx",
    )(*args)


def _mm_res_kernel(a_ref, w_ref, x_ref, gate_ref, cs_ref, o_ref, *, chunk):
    tm = o_ref.shape[0]
    for c in range(tm // chunk):
        rows = pl.ds(c * chunk, chunk)
        acc = jnp.dot(a_ref[rows, :], w_ref[...], preferred_element_type=F32)
        o_ref[rows, :] = x_ref[rows, :] + gate_ref[...] * (acc * cs_ref[...])


def _mm_res_call(a, w, w_idx, xs, modr, col_scale, layer, gate_chunk, row0, *, tm, tn, grouped=False):
    k = w.shape[-2]
    assert a.shape[0] % tm == 0 and row0 % tm == 0
    n_row_tiles = a.shape[0] // tm
    row_tile0 = row0 // tm
    gate = _mod_row(layer, gate_chunk, tm)
    if grouped:
        assert tn == w.shape[-1]
        a_spec = pl.BlockSpec((tm, k), lambda i, j: (i, j))
        w_spec = pl.BlockSpec((None, None, k, tn), lambda i, j: (w_idx, j, 0, 0))
    else:
        a_spec = pl.BlockSpec((tm, k), lambda i, j: (i, 0))
        w_spec = pl.BlockSpec((None, k, tn), lambda i, j: (w_idx, 0, j))
    x_spec = pl.BlockSpec((tm, tn), lambda i, j: (row_tile0 + i, j))
    vmem = 2 * (tm * k * 2 + k * tn * 2 + 2 * tm * tn * 4) + (12 << 20)
    return pl.pallas_call(
        functools.partial(_mm_res_kernel, chunk=min(tm, 512)),
        grid=(n_row_tiles, D_MODEL // tn),
        in_specs=[
            a_spec,
            w_spec,
            x_spec,
            pl.BlockSpec((None, 1, tn), lambda i, j: (gate(row_tile0 + i), 0, j)),
            pl.BlockSpec((1, tn), lambda i, j: (0, j)),
        ],
        out_specs=x_spec,
        out_shape=jax.ShapeDtypeStruct(xs.shape, F32),
        input_output_aliases={2: 0},
        compiler_params=_params(("arbitrary", "arbitrary"), vmem),
        name="mm_residual",
    )(a, w, xs, modr, col_scale)


def _pool_kernel(x_ref, xp_ref, xn_ref, g_ref, sh_ref, sc_ref, o_ref, hs_ref, *, r, halo, n_lat_tiles):
    i = pl.program_id(0)
    seq_len = jnp.where(i < n_lat_tiles, SEQ, CTX_LEN)
    pos0 = (i * r) & (seq_len - 1)
    is_first = pos0 == 0
    is_last = pos0 + r == seq_len
    g, sh, sc = g_ref[...], sh_ref[...], sc_ref[...]
    hs_ref[0:halo, :] = jnp.where(is_first, 0.0, _norm_mod(xp_ref[...], g, sh, sc))
    hs_ref[halo:halo + r, :] = _norm_mod(x_ref[...], g, sh, sc)
    hs_ref[halo + r:, :] = jnp.where(is_last, 0.0, _norm_mod(xn_ref[...], g, sh, sc))

    rc, cc = 128, 256
    for grp, w in enumerate(POOL_WINDOWS):
        before, after = w // 2, w - 1 - w // 2
        for ri in range(r // rc):
            pos = pos0 + ri * rc + lax.broadcasted_iota(jnp.int32, (rc, 1), 0)
            lo = jnp.maximum(pos - before, 0)
            hi = jnp.minimum(pos + after, seq_len - 1)
            inv_cnt = 1.0 / (hi - lo + 1).astype(F32)
            for ci in range(POOL_GROUP // cc):
                cols = slice(grp * POOL_GROUP + ci * cc, grp * POOL_GROUP + (ci + 1) * cc)
                acc = hs_ref[pl.ds(halo + ri * rc - before, rc), cols]
                for s in range(-before + 1, after + 1):
                    acc = acc + hs_ref[pl.ds(halo + ri * rc + s, rc), cols]
                y = acc * inv_cnt - hs_ref[pl.ds(halo + ri * rc, rc), cols]
                o_ref[pl.ds(ri * rc, rc), cols] = y.astype(o_ref.dtype)


def _pool_call(xs, g, modr, layer, n_rows):
    r, halo = 256, 8
    n_halo_blocks = n_rows // halo
    sh = _mod_row(layer, 0, r)
    sc = _mod_row(layer, 1, r)
    kern = functools.partial(_pool_kernel, r=r, halo=halo, n_lat_tiles=N_LAT // r)
    return pl.pallas_call(
        kern,
        grid=(n_rows // r,),
        in_specs=[
            pl.BlockSpec((r, D_MODEL), lambda i: (i, 0)),
            pl.BlockSpec((halo, D_MODEL), lambda i: (jnp.maximum(i * (r // halo) - 1, 0), 0)),
            pl.BlockSpec((halo, D_MODEL), lambda i: (jnp.minimum((i + 1) * (r // halo), n_halo_blocks - 1), 0)),
            pl.BlockSpec((1, D_MODEL), lambda i: (0, 0)),
            pl.BlockSpec((None, 1, D_MODEL), lambda i: (sh(i), 0, 0)),
            pl.BlockSpec((None, 1, D_MODEL), lambda i: (sc(i), 0, 0)),
        ],
        out_specs=pl.BlockSpec((r, D_MODEL), lambda i: (i, 0)),
        out_shape=jax.ShapeDtypeStruct((n_rows, D_MODEL), BF16),
        scratch_shapes=[pltpu.VMEM((r + 2 * halo, D_MODEL), F32)],
        compiler_params=_params(("arbitrary",), 32 << 20),
        name="norm_mod_pool",
    )(xs, xs, xs, g.reshape(1, D_MODEL), modr, modr)


def _ffn_in_kernel(a_ref, ap_ref, an_ref, wg_ref, wv_ref, cwg_ref, cwv_ref, cbg_ref, cbv_ref, o_ref,
                   aext_ref, u0_ref, u1_ref, *, halo, n_col_tiles, n_lat_tiles, dot_chunks, rc):
    t = pl.program_id(0)
    tm, half = o_ref.shape
    j = jnp.minimum(t, pl.num_programs(0) - 2) % n_col_tiles

    @pl.when(t == 0)
    def _():
        u1_ref[...] = jnp.zeros(u1_ref.shape, F32)

    @pl.when(j == 0)
    def _():
        aext_ref[0:halo, :] = ap_ref[...]
        aext_ref[halo:halo + tm, :] = a_ref[...]
        aext_ref[halo + tm:, :] = an_ref[...]

    ie = jnp.maximum(t - 1, 0) // n_col_tiles
    seq_len = jnp.where(ie < n_lat_tiles, SEQ, CTX_LEN)
    sub = lax.broadcasted_iota(jnp.int32, (8, 1), 0)

    def conv(ur_ref, cols, r0, cw_ref, cb_ref, ccols):
        up = ur_ref[pl.ds(halo + r0 - 1, rc), cols]
        un = ur_ref[pl.ds(halo + r0 + 1, rc), cols]
        uc = ur_ref[pl.ds(halo + r0, rc), cols]
        if r0 % CTX_LEN == 0:
            pos = (ie * tm + r0 + sub) & (seq_len - 1)
            up = jnp.concatenate([jnp.where(pos != 0, up[0:8], 0.0), up[8:]], axis=0)
        if (r0 + rc) % CTX_LEN == 0:
            pos = (ie * tm + r0 + rc - 8 + sub) & (seq_len - 1)
            un = jnp.concatenate([un[:rc - 8], jnp.where(pos != seq_len - 1, un[rc - 8:], 0.0)], axis=0)
        return (up * cw_ref[0:1, ccols] + uc * cw_ref[1:2, ccols] + un * cw_ref[2:3, ccols]
                + cb_ref[:, ccols])

    def work(uw_ref, ur_ref):
        n_pieces = tm // rc
        per_chunk = n_pieces // len(dot_chunks)
        row = 0
        for c, size in enumerate(dot_chunks):
            for ri in range(c * per_chunk, (c + 1) * per_chunk):
                r0 = ri * rc
                for ci in range(half // LANES):
                    ccols = slice(ci * LANES, (ci + 1) * LANES)
                    gate = conv(ur_ref, ccols, r0, cwg_ref, cbg_ref, ccols)
                    val = conv(ur_ref, slice(half + ci * LANES, half + (ci + 1) * LANES), r0, cwv_ref,
                               cbv_ref, ccols)
                    o_ref[pl.ds(r0, rc), ccols] = (
                        gate * (1.0 / (1.0 + jnp.exp(-gate))) * val).astype(o_ref.dtype)
            rows = pl.ds(row, size)
            a = aext_ref[rows, :]
            uw_ref[rows, 0:half] = jnp.dot(a, wg_ref[...], preferred_element_type=F32)
            uw_ref[rows, half:] = jnp.dot(a, wv_ref[...], preferred_element_type=F32)
            row += size

    @pl.when(t % 2 == 0)
    def _():
        work(u0_ref, u1_ref)

    @pl.when(t % 2 == 1)
    def _():
        work(u1_ref, u0_ref)


def _ffn_in_call(h, w_in, conv_w, conv_b, layer, n_rows):
    half, halo = 512, BF16_SUBLANES
    n_halo_blocks = n_rows // halo
    ext = TM + 2 * halo
    nj = D_FF // half
    n_steps = (n_rows // TM) * nj
    kern = functools.partial(_ffn_in_kernel, halo=halo, n_col_tiles=nj, n_lat_tiles=N_LAT // TM,
                             dot_chunks=(528, 528), rc=32)
    vmem = (2 * (TM * D_MODEL * 2 + 2 * D_MODEL * half * 2 + TM * half * 2 + 2 * halo * D_MODEL * 2)
            + ext * D_MODEL * 2 + 2 * ext * 2 * half * 4 + (12 << 20))

    def mm(t):
        tt = jnp.minimum(t, n_steps - 1)
        return tt // nj, tt % nj

    def ep(t):
        te = jnp.maximum(t - 1, 0)
        return te // nj, te % nj

    return pl.pallas_call(
        kern,
        grid=(n_steps + 1,),
        in_specs=[
            pl.BlockSpec((TM, D_MODEL), lambda t: (mm(t)[0], 0)),
            pl.BlockSpec((halo, D_MODEL), lambda t: (jnp.maximum(mm(t)[0] * (TM // halo) - 1, 0), 0)),
            pl.BlockSpec((halo, D_MODEL),
                         lambda t: (jnp.minimum((mm(t)[0] + 1) * (TM // halo), n_halo_blocks - 1), 0)),
            pl.BlockSpec((None, D_MODEL, half), lambda t: (layer, 0, mm(t)[1])),
            pl.BlockSpec((None, D_MODEL, half), lambda t: (layer, 0, nj + mm(t)[1])),
            pl.BlockSpec((None, 3, half), lambda t: (layer, 0, ep(t)[1])),
            pl.BlockSpec((None, 3, half), lambda t: (layer, 0, nj + ep(t)[1])),
            pl.BlockSpec((None, 1, half), lambda t: (layer, 0, ep(t)[1])),
            pl.BlockSpec((None, 1, half), lambda t: (layer, 0, nj + ep(t)[1])),
        ],
        out_specs=pl.BlockSpec((TM, half), lambda t: ep(t)),
        out_shape=jax.ShapeDtypeStruct((n_rows, D_FF), BF16),
        scratch_shapes=[
            pltpu.VMEM((ext, D_MODEL), BF16),
            pltpu.VMEM((ext, 2 * half), F32),
            pltpu.VMEM((ext, 2 * half), F32),
        ],
        compiler_params=_params(("arbitrary",), vmem),
        name="ffn_in_conv_glu",
    )(h, h, h, w_in, w_in, conv_w, conv_w, conv_b, conv_b)


def _rope_tables():
    rows = SEQ // GRID_W
    row = jnp.repeat(jnp.arange(rows), GRID_W).astype(F32)
    col = jnp.tile(jnp.arange(GRID_W), rows).astype(F32)
    inv = 1.0 / (ROPE_BASE ** (jnp.arange(0, ROPE_HALF, 2, dtype=F32) / ROPE_HALF))

    def tab(pos):
        a = pos[:, None] * inv[None, :]
        a = jnp.concatenate([a, a], axis=-1)
        return jnp.cos(a), jnp.sin(a)

    cr, sr = tab(row)
    cc, sc = tab(col)
    cos = jnp.concatenate([cr, cc], axis=-1)
    sin = jnp.concatenate([sr, sc], axis=-1)
    lane = jnp.arange(HEAD_DIM)
    sin = jnp.where((lane % ROPE_HALF) < ROPE_HALF // 2, -sin, sin)
    cos_all = jnp.concatenate([jnp.tile(cos, (BATCH, 1)), jnp.ones((N_CTX, HEAD_DIM), F32)], axis=0)
    sin_all = jnp.concatenate([jnp.tile(sin, (BATCH, 1)), jnp.zeros((N_CTX, HEAD_DIM), F32)], axis=0)
    return cos_all, sin_all


def _ctx_needed_after(i):
    return any((j % N_MIXERS) == 0 for j in range(i + 1, DEPTH))


def kernel(x, c, ctx, c_ctx, ada_w, ada_b, norm_g, attn_qkv, attn_o, attn_lambda, attn_subln_g,
           pool_w, pool_scale, ffn_w_in, ffn_conv_w, ffn_conv_b, ffn_w_out, final_g):
    assert x.shape == (BATCH, SEQ, D_MODEL) and ctx.shape == (BATCH, CTX_LEN, D_MODEL)
    xs = jnp.concatenate([x.reshape(N_LAT, D_MODEL), ctx.reshape(N_CTX, D_MODEL)], axis=0)
    cond = jnp.concatenate(
        [c, c_ctx[None, :], jnp.zeros((COND_ROWS - BATCH - 1, D_MODEL), F32)], axis=0)
    mod = _ada(cond, ada_w, ada_b)
    modr = mod.reshape(DEPTH, COND_ROWS, 6, D_MODEL).transpose(0, 2, 1, 3).reshape(
        DEPTH * 6 * COND_ROWS, 1, D_MODEL)
    cos_tab, sin_tab = _rope_tables()
    ones_row = jnp.ones((1, D_MODEL), F32)

    w_qkv = attn_qkv.astype(BF16)
    w_o = attn_o.astype(BF16)
    w_pool = pool_w.astype(BF16)
    w_in = ffn_w_in.astype(BF16)
    conv_b = ffn_conv_b.reshape(DEPTH, 1, 2 * D_FF)
    w_out = ffn_w_out.astype(BF16)

    for i in range(DEPTH):
        need_ctx = _ctx_needed_after(i)
        n_rows = N_ALL if need_ctx else N_LAT
        if i % N_MIXERS == 0:
            a = i // N_MIXERS
            lam_init = 0.8 - 0.6 * math.exp(-0.3 * i)
            h = _norm_mod_call(xs, norm_g[i, 0], modr, i, 0, N_ALL)
            qkv = _qkv_call(h, w_qkv, a, cos_tab, sin_tab, N_ALL)
            o_lat = _attn_call(qkv, attn_lambda[a], attn_subln_g[a], lam_init, latent=True)
            xs = _mm_res_call(o_lat, w_o, a, xs, modr, ones_row, i, 2, 0, tm=TM, tn=512)
            if need_ctx:
                o_ctx = _attn_call(qkv, attn_lambda[a], attn_subln_g[a], lam_init, latent=False)
                xs = _mm_res_call(o_ctx, w_o, a, xs, modr, ones_row, i, 2, N_LAT, tm=TM, tn=512)
        else:
            p = i // N_MIXERS
            y = _pool_call(xs, norm_g[i, 0], modr, i, n_rows)
            xs = _mm_res_call(y, w_pool, p, xs, modr, pool_scale[p].reshape(1, D_MODEL), i, 2, 0,
                              tm=TM, tn=POOL_GROUP, grouped=True)
        h = _norm_mod_call(xs, norm_g[i, 1], modr, i, 3, n_rows)
        act = _ffn_in_call(h, w_in, ffn_conv_w, conv_b, i, n_rows)
        xs = _mm_res_call(act, w_out, i, xs, modr, ones_row, i, 5, 0, tm=TM, tn=512)
    return _final_norm(xs, final_g).reshape(BATCH, SEQ, D_MODEL)
```

```python
import functools
import math

import jax
import jax.numpy as jnp
from jax import lax
from jax.experimental import pallas as pl
from jax.experimental.pallas import tpu as pltpu

D_MODEL = 4096
BATCH = 4
SEQ = 4096
DEPTH = 4
GRID_W = 64
CTX_LEN = 256
N_MIXERS = 2
HEAD_DIM = 128
V_HEAD_DIM = 2 * HEAD_DIM
N_HEADS = D_MODEL // V_HEAD_DIM
ROPE_HALF = HEAD_DIM // 2
ROPE_BASE = 10000.0
POOL_WINDOWS = (2, 4, 8, 16)
N_POOL_GROUPS = len(POOL_WINDOWS)
POOL_GROUP = D_MODEL // N_POOL_GROUPS
D_FF = 7168
EPS = 1e-6
SUBLN_EPS = 1e-5

N_LAT = BATCH * SEQ
N_CTX = BATCH * CTX_LEN
N_ALL = N_LAT + N_CTX
COND_ROWS = 8

V7X_VMEM_BYTES = 64 * 1024 * 1024
VMEM_CAP_BYTES = V7X_VMEM_BYTES - 6 * 1024 * 1024
LANES = 128
BF16_SUBLANES = 16
LOG2_E = math.log2(math.e)

TM = 1024
BF16 = jnp.bfloat16
F32 = jnp.float32


def _params(semantics, vmem_bytes, flags=None):
    return pltpu.CompilerParams(
        dimension_semantics=semantics,
        vmem_limit_bytes=int(min(VMEM_CAP_BYTES, vmem_bytes)),
        flags=flags,
    )


def _mod_row(layer, chunk, tile_rows):
    n_lat_tiles = N_LAT // tile_rows
    tiles_per_seq = SEQ // tile_rows
    base = (layer * 6 + chunk) * COND_ROWS

    def idx(i):
        return base + jnp.where(i < n_lat_tiles, i // tiles_per_seq, BATCH)

    return idx


def _ada_kernel(cond_ref, w_ref, b_ref, o_ref):
    k = pl.program_id(2)
    c = cond_ref[...]
    c = c * (1.0 / (1.0 + jnp.exp(-c)))
    acc = jnp.dot(c.astype(BF16), w_ref[...].astype(BF16), preferred_element_type=F32)

    @pl.when(k == 0)
    def _():
        o_ref[...] = acc + b_ref[...]

    @pl.when(k != 0)
    def _():
        o_ref[...] += acc


def _ada(cond, ada_w, ada_b):
    tn, tk = 2048, 1024
    n = 6 * D_MODEL
    return pl.pallas_call(
        _ada_kernel,
        grid=(DEPTH, n // tn, D_MODEL // tk),
        in_specs=[
            pl.BlockSpec((COND_ROWS, tk), lambda i, j, k: (0, k)),
            pl.BlockSpec((None, tk, tn), lambda i, j, k: (i, k, j)),
            pl.BlockSpec((None, 1, tn), lambda i, j, k: (i, 0, j)),
        ],
        out_specs=pl.BlockSpec((None, COND_ROWS, tn), lambda i, j, k: (i, 0, j)),
        out_shape=jax.ShapeDtypeStruct((DEPTH, COND_ROWS, n), F32),
        compiler_params=_params(("arbitrary", "arbitrary", "arbitrary"), 3 * tk * tn * 4 + (8 << 20)),
        name="ada_mod",
    )(cond, ada_w, ada_b.reshape(DEPTH, 1, n))


def _norm_mod(x, g, sh, sc):
    ms = jnp.mean(x * x, axis=-1, keepdims=True)
    y = x * lax.rsqrt(ms + EPS) * g
    return y * (1.0 + sc) + sh


def _norm_mod_kernel(x_ref, g_ref, sh_ref, sc_ref, o_ref):
    o_ref[...] = _norm_mod(x_ref[...], g_ref[...], sh_ref[...], sc_ref[...]).astype(o_ref.dtype)


def _norm_mod_call(xs, g, modr, layer, chunk0, n_rows):
    r = 512
    sh = _mod_row(layer, chunk0, r)
    sc = _mod_row(layer, chunk0 + 1, r)
    return pl.pallas_call(
        _norm_mod_kernel,
        grid=(n_rows // r,),
        in_specs=[
            pl.BlockSpec((r, D_MODEL), lambda i: (i, 0)),
            pl.BlockSpec((1, D_MODEL), lambda i: (0, 0)),
            pl.BlockSpec((None, 1, D_MODEL), lambda i: (sh(i), 0, 0)),
            pl.BlockSpec((None, 1, D_MODEL), lambda i: (sc(i), 0, 0)),
        ],
        out_specs=pl.BlockSpec((r, D_MODEL), lambda i: (i, 0)),
        out_shape=jax.ShapeDtypeStruct((n_rows, D_MODEL), BF16),
        compiler_params=_params(("arbitrary",), 6 * r * D_MODEL * 4),
        name="norm_mod",
    )(xs, g.reshape(1, D_MODEL), modr, modr)


def _final_norm_kernel(x_ref, g_ref, o_ref):
    x = x_ref[...]
    ms = jnp.mean(x * x, axis=-1, keepdims=True)
    o_ref[...] = x * lax.rsqrt(ms + EPS) * g_ref[...]


def _final_norm(xs, g):
    r = 512
    return pl.pallas_call(
        _final_norm_kernel,
        grid=(N_LAT // r,),
        in_specs=[
            pl.BlockSpec((r, D_MODEL), lambda i: (i, 0)),
            pl.BlockSpec((1, D_MODEL), lambda i: (0, 0)),
        ],
        out_specs=pl.BlockSpec((r, D_MODEL), lambda i: (i, 0)),
        out_shape=jax.ShapeDtypeStruct((N_LAT, D_MODEL), F32),
        compiler_params=_params(("arbitrary",), 6 * r * D_MODEL * 4),
        name="final_norm",
    )(xs, g.reshape(1, D_MODEL))


def _qkv_kernel(a_ref, w_ref, cos_ref, sin_ref, o_ref, u0_ref, u1_ref, *, n_col_tiles, dot_chunks, rc,
                n_rope_blocks, n_q_blocks, q_scale):
    t = pl.program_id(0)
    tm, tn = o_ref.shape
    je = jnp.maximum(t - 1, 0) % n_col_tiles
    use_rope = je < n_rope_blocks
    qs = jnp.where(je < n_q_blocks, q_scale, 1.0).astype(F32)
    lane = lax.broadcasted_iota(jnp.int32, (rc, HEAD_DIM), 1)
    first_half = (lane % ROPE_HALF) < (ROPE_HALF // 2)

    @pl.when(t == 0)
    def _():
        u1_ref[...] = jnp.zeros(u1_ref.shape, F32)

    def work(uw_ref, ur_ref):
        per_chunk = (tm // rc) // len(dot_chunks)
        row = 0
        for c, size in enumerate(dot_chunks):
            for ri in range(c * per_chunk, (c + 1) * per_chunk):
                rows = pl.ds(ri * rc, rc)
                cos = jnp.where(use_rope, cos_ref[rows, :], 1.0)
                sin = jnp.where(use_rope, sin_ref[rows, :], 0.0)
                for ci in range(tn // HEAD_DIM):
                    cols = slice(ci * HEAD_DIM, (ci + 1) * HEAD_DIM)
                    x = ur_ref[rows, cols]
                    rot = jnp.where(first_half, pltpu.roll(x, HEAD_DIM - ROPE_HALF // 2, 1),
                                    pltpu.roll(x, ROPE_HALF // 2, 1))
                    o_ref[rows, cols] = ((x * cos + rot * sin) * qs).astype(o_ref.dtype)
            rows = pl.ds(row, size)
            uw_ref[rows, :] = jnp.dot(a_ref[rows, :], w_ref[...], preferred_element_type=F32)
            row += size

    @pl.when(t % 2 == 0)
    def _():
        work(u0_ref, u1_ref)

    @pl.when(t % 2 == 1)
    def _():
        work(u1_ref, u0_ref)


def _qkv_call(h, w, layer, cos_tab, sin_tab, n_rows):
    tn = 1024
    n = 3 * D_MODEL
    nj = n // tn
    n_steps = (n_rows // TM) * nj
    kern = functools.partial(
        _qkv_kernel, n_col_tiles=nj, dot_chunks=(256, 256, 256, 256), rc=64,
        n_rope_blocks=2 * D_MODEL // tn, n_q_blocks=D_MODEL // tn, q_scale=HEAD_DIM ** -0.5 * LOG2_E)
    vmem = (2 * (TM * D_MODEL * 2 + D_MODEL * tn * 2 + TM * tn * 2 + 2 * TM * HEAD_DIM * 4)
            + 2 * TM * tn * 4 + (12 << 20))

    def mm(t):
        tt = jnp.minimum(t, n_steps - 1)
        return tt // nj, tt % nj

    def ep(t):
        te = jnp.maximum(t - 1, 0)
        return te // nj, te % nj

    return pl.pallas_call(
        kern,
        grid=(n_steps + 1,),
        in_specs=[
            pl.BlockSpec((TM, D_MODEL), lambda t: (mm(t)[0], 0)),
            pl.BlockSpec((None, D_MODEL, tn), lambda t: (layer, 0, mm(t)[1])),
            pl.BlockSpec((TM, HEAD_DIM), lambda t: (ep(t)[0], 0)),
            pl.BlockSpec((TM, HEAD_DIM), lambda t: (ep(t)[0], 0)),
        ],
        out_specs=pl.BlockSpec((TM, tn), lambda t: ep(t)),
        out_shape=jax.ShapeDtypeStruct((n_rows, n), BF16),
        scratch_shapes=[pltpu.VMEM((TM, tn), F32), pltpu.VMEM((TM, tn), F32)],
        compiler_params=_params(("arbitrary",), vmem),
        name="qkv_rope",
    )(h, w, cos_tab, sin_tab)


def _attn_kernel(lam_ref, g_ref, q_ref, *refs, tq, tk, n_ctx_keys, n_lat_keys, n_casts, lam_init):
    refs = list(refs)
    kc_ref, vc_ref = refs[:2]
    del refs[:2]
    if n_lat_keys:
        kl_ref, vl_ref = refs[:2]
        del refs[:2]
    cast_in = refs[:n_casts]
    o_ref = refs[n_casts]
    cast_out = refs[n_casts + 1:2 * n_casts + 1]
    m_ref, l_ref, acc_ref = refs[2 * n_casts + 1:]
    for src, dst in zip(cast_in, cast_out):
        dst[...] = src[...].astype(dst.dtype)
    m_ref[...] = jnp.full(m_ref.shape, -jnp.inf, F32)
    l_ref[...] = jnp.zeros(l_ref.shape, F32)
    acc_ref[...] = jnp.zeros(acc_ref.shape, F32)

    def scores(k):
        out = []
        for m in range(2):
            cols = slice(m * HEAD_DIM, (m + 1) * HEAD_DIM)
            out.append(lax.dot_general(q_ref[:, cols], k[:, cols], (((1,), (1,)), ((), ())),
                                       preferred_element_type=F32))
        return out

    def softmax_pv(s_maps, v):
        for m, s in enumerate(s_maps):
            n = s.shape[1]
            m_prev = m_ref[m]
            m_next = jnp.maximum(m_prev, jnp.max(s, axis=1, keepdims=True))
            alpha = jnp.exp2(m_prev - m_next)
            p = jnp.exp2(s - jnp.concatenate([m_next] * (n // LANES), axis=1))
            l_ref[m] = alpha * l_ref[m] + jnp.sum(p, axis=1, keepdims=True)
            m_ref[m] = m_next
            pv = jnp.dot(p.astype(BF16), v, preferred_element_type=F32)
            acc_ref[m] = acc_ref[m] * jnp.concatenate([alpha] * (V_HEAD_DIM // LANES), axis=1) + pv

    chunks = [(kc_ref, vc_ref, pl.ds(0, n_ctx_keys))]
    chunks += [(kl_ref, vl_ref, pl.ds(c * tk, tk)) for c in range(n_lat_keys // tk)]
    s_next = scores(chunks[0][0][chunks[0][2], :])
    for c, (_, v_ref, rows) in enumerate(chunks):
        s_cur = s_next
        if c + 1 < len(chunks):
            k_next, _, rows_next = chunks[c + 1]
            s_next = scores(k_next[rows_next, :])
        softmax_pv(s_cur, v_ref[rows, :])

    lp = lam_ref[...]
    lam = (jnp.exp(jnp.sum(lp[0:1] * lp[1:2], axis=1, keepdims=True))
           - jnp.exp(jnp.sum(lp[2:3] * lp[3:4], axis=1, keepdims=True)) + lam_init)
    rep = V_HEAD_DIM // LANES
    o1 = acc_ref[0] / jnp.concatenate([l_ref[0]] * rep, axis=1)
    o2 = acc_ref[1] / jnp.concatenate([l_ref[1]] * rep, axis=1)
    o = o1 - lam * o2
    ms = jnp.mean(o * o, axis=-1, keepdims=True)
    y = o * lax.rsqrt(ms + SUBLN_EPS) * g_ref[...]
    o_ref[...] = (y * (1.0 - lam_init)).astype(o_ref.dtype)


def _attn_call(qkv, lam_p, subln_g, lam_init, latent, casts=()):
    hq = D_MODEL // V_HEAD_DIM
    ctx_row0 = N_LAT // CTX_LEN
    if latent:
        tq, tk, n_lat_keys, n_out = 512, 1024, SEQ, N_LAT
        n_q = SEQ // tq
        q_map = lambda b, h, qi: (b * (SEQ // tq) + qi, h)
    else:
        tq, tk, n_q, n_lat_keys, n_out = CTX_LEN, 512, 1, 0, N_CTX
        q_map = lambda b, h, qi: (ctx_row0 + b, h)
    in_specs = [
        pl.BlockSpec((4, HEAD_DIM), lambda b, h, qi: (0, 0)),
        pl.BlockSpec((1, V_HEAD_DIM), lambda b, h, qi: (0, 0)),
        pl.BlockSpec((tq, V_HEAD_DIM), q_map),
        pl.BlockSpec((CTX_LEN, V_HEAD_DIM), lambda b, h, qi: (ctx_row0 + b, hq + h)),
        pl.BlockSpec((CTX_LEN, V_HEAD_DIM), lambda b, h, qi: (ctx_row0 + b, 2 * hq + h)),
    ]
    args = [lam_p, subln_g.reshape(1, V_HEAD_DIM), qkv, qkv, qkv]
    if latent:
        in_specs += [
            pl.BlockSpec((SEQ, V_HEAD_DIM), lambda b, h, qi: (b, hq + h)),
            pl.BlockSpec((SEQ, V_HEAD_DIM), lambda b, h, qi: (b, 2 * hq + h)),
        ]
        args += [qkv, qkv]
        out_map = lambda b, h, qi: (b * (SEQ // tq) + qi, h)
    else:
        out_map = lambda b, h, qi: (b, h)
    n_steps = BATCH * N_HEADS * n_q
    slab_rows = n_steps * BF16_SUBLANES
    out_specs = [pl.BlockSpec((tq, V_HEAD_DIM), out_map)]
    out_shape = [jax.ShapeDtypeStruct((n_out, D_MODEL), BF16)]
    cast_shapes = []
    for w, layer in casts:
        slab = math.prod(w.shape[1:])
        assert slab % slab_rows == 0
        cols = slab // slab_rows
        step_block = lambda b, h, qi, layer=layer: (layer * n_steps + (b * N_HEADS + h) * n_q + qi, 0)
        in_specs.append(pl.BlockSpec((BF16_SUBLANES, cols), step_block))
        args.append(w.reshape(w.shape[0] * slab_rows, cols))
        out_specs.append(pl.BlockSpec((BF16_SUBLANES, cols),
                                      lambda b, h, qi: ((b * N_HEADS + h) * n_q + qi, 0)))
        out_shape.append(jax.ShapeDtypeStruct((slab_rows, cols), BF16))
        cast_shapes.append((1,) + w.shape[1:])
    kern = functools.partial(_attn_kernel, tq=tq, tk=tk, n_ctx_keys=CTX_LEN, n_lat_keys=n_lat_keys,
                             n_casts=len(casts), lam_init=lam_init)
    outs = pl.pallas_call(
        kern,
        grid=(BATCH, N_HEADS, n_q),
        in_specs=in_specs,
        out_specs=out_specs,
        out_shape=out_shape,
        scratch_shapes=[
            pltpu.VMEM((2, tq, LANES), F32),
            pltpu.VMEM((2, tq, LANES), F32),
            pltpu.VMEM((2, tq, V_HEAD_DIM), F32),
        ],
        compiler_params=_params(("arbitrary", "arbitrary", "arbitrary"), 48 << 20),
        name="diff_attn_lat" if latent else "diff_attn_ctx",
    )(*args)
    return outs[0], [o.reshape(s) for o, s in zip(outs[1:], cast_shapes)]


def _mm_res_kernel(a_ref, w_ref, x_ref, gate_ref, cs_ref, o_ref, *, chunk):
    tm = o_ref.shape[0]
    for c in range(tm // chunk):
        rows = pl.ds(c * chunk, chunk)
        acc = jnp.dot(a_ref[rows, :], w_ref[...], preferred_element_type=F32)
        o_ref[rows, :] = x_ref[rows, :] + gate_ref[...] * (acc * cs_ref[...])


def _mm_res_call(a, w, w_idx, xs, modr, col_scale, layer, gate_chunk, row0, *, tm, tn, grouped=False):
    k = w.shape[-2]
    assert a.shape[0] % tm == 0 and row0 % tm == 0
    n_row_tiles = a.shape[0] // tm
    row_tile0 = row0 // tm
    gate = _mod_row(layer, gate_chunk, tm)
    if grouped:
        assert tn == w.shape[-1]
        a_spec = pl.BlockSpec((tm, k), lambda i, j: (i, j))
        w_spec = pl.BlockSpec((None, None, k, tn), lambda i, j: (w_idx, j, 0, 0))
    else:
        a_spec = pl.BlockSpec((tm, k), lambda i, j: (i, 0))
        w_spec = pl.BlockSpec((None, k, tn), lambda i, j: (w_idx, 0, j))
    x_spec = pl.BlockSpec((tm, tn), lambda i, j: (row_tile0 + i, j))
    vmem = 2 * (tm * k * 2 + k * tn * 2 + 2 * tm * tn * 4) + (12 << 20)
    return pl.pallas_call(
        functools.partial(_mm_res_kernel, chunk=min(tm, 512)),
        grid=(n_row_tiles, D_MODEL // tn),
        in_specs=[
            a_spec,
            w_spec,
            x_spec,
            pl.BlockSpec((None, 1, tn), lambda i, j: (gate(row_tile0 + i), 0, j)),
            pl.BlockSpec((1, tn), lambda i, j: (0, j)),
        ],
        out_specs=x_spec,
        out_shape=jax.ShapeDtypeStruct(xs.shape, F32),
        input_output_aliases={2: 0},
        compiler_params=_params(("arbitrary", "arbitrary"), vmem),
        name="mm_residual",
    )(a, w, xs, modr, col_scale)


def _pool_kernel(x_ref, xp_ref, xn_ref, g_ref, sh_ref, sc_ref, o_ref, hs_ref, *, r, halo, n_lat_tiles):
    i = pl.program_id(0)
    seq_len = jnp.where(i < n_lat_tiles, SEQ, CTX_LEN)
    pos0 = (i * r) & (seq_len - 1)
    is_first = pos0 == 0
    is_last = pos0 + r == seq_len
    g, sh, sc = g_ref[...], sh_ref[...], sc_ref[...]
    hs_ref[0:halo, :] = jnp.where(is_first, 0.0, _norm_mod(xp_ref[...], g, sh, sc))
    hs_ref[halo:halo + r, :] = _norm_mod(x_ref[...], g, sh, sc)
    hs_ref[halo + r:, :] = jnp.where(is_last, 0.0, _norm_mod(xn_ref[...], g, sh, sc))

    rc, cc = 128, 256
    for grp, w in enumerate(POOL_WINDOWS):
        before, after = w // 2, w - 1 - w // 2
        for ri in range(r // rc):
            pos = pos0 + ri * rc + lax.broadcasted_iota(jnp.int32, (rc, 1), 0)
            lo = jnp.maximum(pos - before, 0)
            hi = jnp.minimum(pos + after, seq_len - 1)
            inv_cnt = 1.0 / (hi - lo + 1).astype(F32)
            for ci in range(POOL_GROUP // cc):
                cols = slice(grp * POOL_GROUP + ci * cc, grp * POOL_GROUP + (ci + 1) * cc)
                acc = hs_ref[pl.ds(halo + ri * rc - before, rc), cols]
                for s in range(-before + 1, after + 1):
                    acc = acc + hs_ref[pl.ds(halo + ri * rc + s, rc), cols]
                y = acc * inv_cnt - hs_ref[pl.ds(halo + ri * rc, rc), cols]
                o_ref[pl.ds(ri * rc, rc), cols] = y.astype(o_ref.dtype)


def _pool_call(xs, g, modr, layer, n_rows):
    r, halo = 256, 8
    n_halo_blocks = n_rows // halo
    sh = _mod_row(layer, 0, r)
    sc = _mod_row(layer, 1, r)
    kern = functools.partial(_pool_kernel, r=r, halo=halo, n_lat_tiles=N_LAT // r)
    return pl.pallas_call(
        kern,
        grid=(n_rows // r,),
        in_specs=[
            pl.BlockSpec((r, D_MODEL), lambda i: (i, 0)),
            pl.BlockSpec((halo, D_MODEL), lambda i: (jnp.maximum(i * (r // halo) - 1, 0), 0)),
            pl.BlockSpec((halo, D_MODEL), lambda i: (jnp.minimum((i + 1) * (r // halo), n_halo_blocks - 1), 0)),
            pl.BlockSpec((1, D_MODEL), lambda i: (0, 0)),
            pl.BlockSpec((None, 1, D_MODEL), lambda i: (sh(i), 0, 0)),
            pl.BlockSpec((None, 1, D_MODEL), lambda i: (sc(i), 0, 0)),
        ],
        out_specs=pl.BlockSpec((r, D_MODEL), lambda i: (i, 0)),
        out_shape=jax.ShapeDtypeStruct((n_rows, D_MODEL), BF16),
        scratch_shapes=[pltpu.VMEM((r + 2 * halo, D_MODEL), F32)],
        compiler_params=_params(("arbitrary",), 32 << 20),
        name="norm_mod_pool",
    )(xs, xs, xs, g.reshape(1, D_MODEL), modr, modr)


def _ffn_in_kernel(a_ref, ap_ref, an_ref, wg_ref, wv_ref, cwg_ref, cwv_ref, cbg_ref, cbv_ref, o_ref,
                   aext_ref, u0_ref, u1_ref, *, halo, n_col_tiles, n_lat_tiles, dot_chunks, rc):
    t = pl.program_id(0)
    tm, half = o_ref.shape
    j = jnp.minimum(t, pl.num_programs(0) - 2) % n_col_tiles

    @pl.when(t == 0)
    def _():
        u1_ref[...] = jnp.zeros(u1_ref.shape, F32)

    @pl.when(j == 0)
    def _():
        aext_ref[0:halo, :] = ap_ref[...]
        aext_ref[halo:halo + tm, :] = a_ref[...]
        aext_ref[halo + tm:, :] = an_ref[...]

    ie = jnp.maximum(t - 1, 0) // n_col_tiles
    seq_len = jnp.where(ie < n_lat_tiles, SEQ, CTX_LEN)
    sub = lax.broadcasted_iota(jnp.int32, (8, 1), 0)

    def conv(ur_ref, cols, r0, cw_ref, cb_ref, ccols):
        up = ur_ref[pl.ds(halo + r0 - 1, rc), cols]
        un = ur_ref[pl.ds(halo + r0 + 1, rc), cols]
        uc = ur_ref[pl.ds(halo + r0, rc), cols]
        if r0 % CTX_LEN == 0:
            pos = (ie * tm + r0 + sub) & (seq_len - 1)
            up = jnp.concatenate([jnp.where(pos != 0, up[0:8], 0.0), up[8:]], axis=0)
        if (r0 + rc) % CTX_LEN == 0:
            pos = (ie * tm + r0 + rc - 8 + sub) & (seq_len - 1)
            un = jnp.concatenate([un[:rc - 8], jnp.where(pos != seq_len - 1, un[rc - 8:], 0.0)], axis=0)
        return (up * cw_ref[0:1, ccols] + uc * cw_ref[1:2, ccols] + un * cw_ref[2:3, ccols]
                + cb_ref[:, ccols])

    def work(uw_ref, ur_ref):
        n_pieces = tm // rc
        per_chunk = n_pieces // len(dot_chunks)
        row = 0
        for c, size in enumerate(dot_chunks):
            for ri in range(c * per_chunk, (c + 1) * per_chunk):
                r0 = ri * rc
                for ci in range(half // LANES):
                    ccols = slice(ci * LANES, (ci + 1) * LANES)
                    gate = conv(ur_ref, ccols, r0, cwg_ref, cbg_ref, ccols)
                    val = conv(ur_ref, slice(half + ci * LANES, half + (ci + 1) * LANES), r0, cwv_ref,
                               cbv_ref, ccols)
                    o_ref[pl.ds(r0, rc), ccols] = (
                        gate * (1.0 / (1.0 + jnp.exp(-gate))) * val).astype(o_ref.dtype)
            rows = pl.ds(row, size)
            a = aext_ref[rows, :]
            uw_ref[rows, 0:half] = jnp.dot(a, wg_ref[...], preferred_element_type=F32)
            uw_ref[rows, half:] = jnp.dot(a, wv_ref[...], preferred_element_type=F32)
            row += size

    @pl.when(t % 2 == 0)
    def _():
        work(u0_ref, u1_ref)

    @pl.when(t % 2 == 1)
    def _():
        work(u1_ref, u0_ref)


def _ffn_in_call(h, w_in, w_idx, conv_w, conv_b, layer, n_rows):
    half, halo = 512, BF16_SUBLANES
    n_halo_blocks = n_rows // halo
    ext = TM + 2 * halo
    nj = D_FF // half
    n_steps = (n_rows // TM) * nj
    kern = functools.partial(_ffn_in_kernel, halo=halo, n_col_tiles=nj, n_lat_tiles=N_LAT // TM,
                             dot_chunks=(528, 528), rc=32)
    vmem = (2 * (TM * D_MODEL * 2 + 2 * D_MODEL * half * 2 + TM * half * 2 + 2 * halo * D_MODEL * 2)
            + ext * D_MODEL * 2 + 2 * ext * 2 * half * 4 + (12 << 20))

    def mm(t):
        tt = jnp.minimum(t, n_steps - 1)
        return tt // nj, tt % nj

    def ep(t):
        te = jnp.maximum(t - 1, 0)
        return te // nj, te % nj

    return pl.pallas_call(
        kern,
        grid=(n_steps + 1,),
        in_specs=[
            pl.BlockSpec((TM, D_MODEL), lambda t: (mm(t)[0], 0)),
            pl.BlockSpec((halo, D_MODEL), lambda t: (jnp.maximum(mm(t)[0] * (TM // halo) - 1, 0), 0)),
            pl.BlockSpec((halo, D_MODEL),
                         lambda t: (jnp.minimum((mm(t)[0] + 1) * (TM // halo), n_halo_blocks - 1), 0)),
            pl.BlockSpec((None, D_MODEL, half), lambda t: (w_idx, 0, mm(t)[1])),
            pl.BlockSpec((None, D_MODEL, half), lambda t: (w_idx, 0, nj + mm(t)[1])),
            pl.BlockSpec((None, 3, half), lambda t: (layer, 0, ep(t)[1])),
            pl.BlockSpec((None, 3, half), lambda t: (layer, 0, nj + ep(t)[1])),
            pl.BlockSpec((None, 1, half), lambda t: (layer, 0, ep(t)[1])),
            pl.BlockSpec((None, 1, half), lambda t: (layer, 0, nj + ep(t)[1])),
        ],
        out_specs=pl.BlockSpec((TM, half), lambda t: ep(t)),
        out_shape=jax.ShapeDtypeStruct((n_rows, D_FF), BF16),
        scratch_shapes=[
            pltpu.VMEM((ext, D_MODEL), BF16),
            pltpu.VMEM((ext, 2 * half), F32),
            pltpu.VMEM((ext, 2 * half), F32),
        ],
        compiler_params=_params(("arbitrary",), vmem),
        name="ffn_in_conv_glu",
    )(h, h, h, w_in, w_in, conv_w, conv_w, conv_b, conv_b)


def _rope_tables():
    rows = SEQ // GRID_W
    row = jnp.repeat(jnp.arange(rows), GRID_W).astype(F32)
    col = jnp.tile(jnp.arange(GRID_W), rows).astype(F32)
    inv = 1.0 / (ROPE_BASE ** (jnp.arange(0, ROPE_HALF, 2, dtype=F32) / ROPE_HALF))

    def tab(pos):
        a = pos[:, None] * inv[None, :]
        a = jnp.concatenate([a, a], axis=-1)
        return jnp.cos(a), jnp.sin(a)

    cr, sr = tab(row)
    cc, sc = tab(col)
    cos = jnp.concatenate([cr, cc], axis=-1)
    sin = jnp.concatenate([sr, sc], axis=-1)
    lane = jnp.arange(HEAD_DIM)
    sin = jnp.where((lane % ROPE_HALF) < ROPE_HALF // 2, -sin, sin)
    cos_all = jnp.concatenate([jnp.tile(cos, (BATCH, 1)), jnp.ones((N_CTX, HEAD_DIM), F32)], axis=0)
    sin_all = jnp.concatenate([jnp.tile(sin, (BATCH, 1)), jnp.zeros((N_CTX, HEAD_DIM), F32)], axis=0)
    return cos_all, sin_all


def _ctx_needed_after(i):
    return any((j % N_MIXERS) == 0 for j in range(i + 1, DEPTH))


def kernel(x, c, ctx, c_ctx, ada_w, ada_b, norm_g, attn_qkv, attn_o, attn_lambda, attn_subln_g,
           pool_w, pool_scale, ffn_w_in, ffn_conv_w, ffn_conv_b, ffn_w_out, final_g):
    assert x.shape == (BATCH, SEQ, D_MODEL) and ctx.shape == (BATCH, CTX_LEN, D_MODEL)
    xs = jnp.concatenate([x.reshape(N_LAT, D_MODEL), ctx.reshape(N_CTX, D_MODEL)], axis=0)
    cond = jnp.concatenate(
        [c, c_ctx[None, :], jnp.zeros((COND_ROWS - BATCH - 1, D_MODEL), F32)], axis=0)
    mod = _ada(cond, ada_w, ada_b)
    modr = mod.reshape(DEPTH, COND_ROWS, 6, D_MODEL).transpose(0, 2, 1, 3).reshape(
        DEPTH * 6 * COND_ROWS, 1, D_MODEL)
    cos_tab, sin_tab = _rope_tables()
    ones_row = jnp.ones((1, D_MODEL), F32)

    conv_b = ffn_conv_b.reshape(DEPTH, 1, 2 * D_FF)
    w16 = {("qkv", 0): attn_qkv[0:1].astype(BF16)}

    for i in range(DEPTH):
        need_ctx = _ctx_needed_after(i)
        n_rows = N_ALL if need_ctx else N_LAT
        if i % N_MIXERS == 0:
            a = i // N_MIXERS
            lam_init = 0.8 - 0.6 * math.exp(-0.3 * i)
            h = _norm_mod_call(xs, norm_g[i, 0], modr, i, 0, N_ALL)
            qkv = _qkv_call(h, w16["qkv", a], 0, cos_tab, sin_tab, N_ALL)
            casts = [("o", a, attn_o), ("in", i, ffn_w_in), ("out", i, ffn_w_out)]
            for nxt in range(i + 1, min(i + N_MIXERS, DEPTH)):
                casts += [("pool", nxt // N_MIXERS, pool_w), ("in", nxt, ffn_w_in), ("out", nxt, ffn_w_out)]
            if i + N_MIXERS < DEPTH:
                casts.append(("qkv", a + 1, attn_qkv))
            o_lat, cast_out = _attn_call(qkv, attn_lambda[a], attn_subln_g[a], lam_init, latent=True,
                                         casts=[(w, idx) for _, idx, w in casts])
            for (name, idx, _), w_bf16 in zip(casts, cast_out):
                w16[name, idx] = w_bf16
            xs = _mm_res_call(o_lat, w16["o", a], 0, xs, modr, ones_row, i, 2, 0, tm=TM, tn=512)
            if need_ctx:
                o_ctx, _ = _attn_call(qkv, attn_lambda[a], attn_subln_g[a], lam_init, latent=False)
                xs = _mm_res_call(o_ctx, w16["o", a], 0, xs, modr, ones_row, i, 2, N_LAT, tm=TM, tn=512)
        else:
            p = i // N_MIXERS
            y = _pool_call(xs, norm_g[i, 0], modr, i, n_rows)
            xs = _mm_res_call(y, w16["pool", p], 0, xs, modr, pool_scale[p].reshape(1, D_MODEL), i, 2, 0,
                              tm=TM, tn=POOL_GROUP, grouped=True)
        h = _norm_mod_call(xs, norm_g[i, 1], modr, i, 3, n_rows)
        act = _ffn_in_call(h, w16["in", i], 0, ffn_conv_w, conv_b, i, n_rows)
        xs = _mm_res_call(act, w16["out", i], 0, xs, modr, ones_row, i, 5, 0, tm=TM, tn=512)
    return _final_norm(xs, final_g).reshape(BATCH, SEQ, D_MODEL)
```

```python
import functools
import math

import jax
import jax.numpy as jnp
from jax import lax
from jax.experimental import pallas as pl
from jax.experimental.pallas import tpu as pltpu

D_MODEL = 4096
BATCH = 4
SEQ = 4096
DEPTH = 4
GRID_W = 64
CTX_LEN = 256
N_MIXERS = 2
HEAD_DIM = 128
V_HEAD_DIM = 2 * HEAD_DIM
N_HEADS = D_MODEL // V_HEAD_DIM
ROPE_HALF = HEAD_DIM // 2
ROPE_BASE = 10000.0
POOL_WINDOWS = (2, 4, 8, 16)
N_POOL_GROUPS = len(POOL_WINDOWS)
POOL_GROUP = D_MODEL // N_POOL_GROUPS
D_FF = 7168
EPS = 1e-6
SUBLN_EPS = 1e-5

N_LAT = BATCH * SEQ
N_CTX = BATCH * CTX_LEN
N_ALL = N_LAT + N_CTX
COND_ROWS = 8

V7X_VMEM_BYTES = 64 * 1024 * 1024
VMEM_CAP_BYTES = V7X_VMEM_BYTES - 6 * 1024 * 1024
LANES = 128
BF16_SUBLANES = 16
LOG2_E = math.log2(math.e)

TM = 1024
BF16 = jnp.bfloat16
F32 = jnp.float32


def _params(semantics, vmem_bytes, flags=None):
    return pltpu.CompilerParams(
        dimension_semantics=semantics,
        vmem_limit_bytes=int(min(VMEM_CAP_BYTES, vmem_bytes)),
        flags=flags,
    )


def _mod_row(layer, chunk, tile_rows):
    n_lat_tiles = N_LAT // tile_rows
    tiles_per_seq = SEQ // tile_rows
    base = (layer * 6 + chunk) * COND_ROWS

    def idx(i):
        return base + jnp.where(i < n_lat_tiles, i // tiles_per_seq, BATCH)

    return idx


def _ada_kernel(cond_ref, w_ref, b_ref, o_ref):
    k = pl.program_id(2)
    c = cond_ref[...]
    c = c * (1.0 / (1.0 + jnp.exp(-c)))
    acc = jnp.dot(c.astype(BF16), w_ref[...].astype(BF16), preferred_element_type=F32)

    @pl.when(k == 0)
    def _():
        o_ref[...] = acc + b_ref[...]

    @pl.when(k != 0)
    def _():
        o_ref[...] += acc


def _ada(cond, ada_w, ada_b):
    tn, tk = 2048, 1024
    n = 6 * D_MODEL
    return pl.pallas_call(
        _ada_kernel,
        grid=(DEPTH, n // tn, D_MODEL // tk),
        in_specs=[
            pl.BlockSpec((COND_ROWS, tk), lambda i, j, k: (0, k)),
            pl.BlockSpec((None, tk, tn), lambda i, j, k: (i, k, j)),
            pl.BlockSpec((None, 1, tn), lambda i, j, k: (i, 0, j)),
        ],
        out_specs=pl.BlockSpec((None, COND_ROWS, tn), lambda i, j, k: (i, 0, j)),
        out_shape=jax.ShapeDtypeStruct((DEPTH, COND_ROWS, n), F32),
        compiler_params=_params(("arbitrary", "arbitrary", "arbitrary"), 3 * tk * tn * 4 + (8 << 20)),
        name="ada_mod",
    )(cond, ada_w, ada_b.reshape(DEPTH, 1, n))


def _norm_mod(x, g, sh, sc):
    ms = jnp.mean(x * x, axis=-1, keepdims=True)
    y = x * lax.rsqrt(ms + EPS) * g
    return y * (1.0 + sc) + sh


def _norm_mod_kernel(x_ref, g_ref, sh_ref, sc_ref, o_ref):
    o_ref[...] = _norm_mod(x_ref[...], g_ref[...], sh_ref[...], sc_ref[...]).astype(o_ref.dtype)


def _norm_mod_call(xs, g, modr, layer, chunk0, n_rows):
    r = 512
    sh = _mod_row(layer, chunk0, r)
    sc = _mod_row(layer, chunk0 + 1, r)
    return pl.pallas_call(
        _norm_mod_kernel,
        grid=(n_rows // r,),
        in_specs=[
            pl.BlockSpec((r, D_MODEL), lambda i: (i, 0)),
            pl.BlockSpec((1, D_MODEL), lambda i: (0, 0)),
            pl.BlockSpec((None, 1, D_MODEL), lambda i: (sh(i), 0, 0)),
            pl.BlockSpec((None, 1, D_MODEL), lambda i: (sc(i), 0, 0)),
        ],
        out_specs=pl.BlockSpec((r, D_MODEL), lambda i: (i, 0)),
        out_shape=jax.ShapeDtypeStruct((n_rows, D_MODEL), BF16),
        compiler_params=_params(("arbitrary",), 6 * r * D_MODEL * 4),
        name="norm_mod",
    )(xs, g.reshape(1, D_MODEL), modr, modr)


def _final_norm_kernel(x_ref, g_ref, o_ref):
    x = x_ref[...]
    ms = jnp.mean(x * x, axis=-1, keepdims=True)
    o_ref[...] = x * lax.rsqrt(ms + EPS) * g_ref[...]


def _final_norm(xs, g):
    r = 512
    return pl.pallas_call(
        _final_norm_kernel,
        grid=(N_LAT // r,),
        in_specs=[
            pl.BlockSpec((r, D_MODEL), lambda i: (i, 0)),
            pl.BlockSpec((1, D_MODEL), lambda i: (0, 0)),
        ],
        out_specs=pl.BlockSpec((r, D_MODEL), lambda i: (i, 0)),
        out_shape=jax.ShapeDtypeStruct((N_LAT, D_MODEL), F32),
        compiler_params=_params(("arbitrary",), 6 * r * D_MODEL * 4),
        name="final_norm",
    )(xs, g.reshape(1, D_MODEL))


def _qkv_kernel(a_ref, w_ref, cos_ref, sin_ref, o_ref, u0_ref, u1_ref, *, n_col_tiles, dot_chunks, rc,
                n_rope_blocks, n_q_blocks, q_scale):
    t = pl.program_id(0)
    tm, tn = o_ref.shape
    je = jnp.maximum(t - 1, 0) % n_col_tiles
    use_rope = je < n_rope_blocks
    qs = jnp.where(je < n_q_blocks, q_scale, 1.0).astype(F32)
    lane = lax.broadcasted_iota(jnp.int32, (rc, HEAD_DIM), 1)
    first_half = (lane % ROPE_HALF) < (ROPE_HALF // 2)

    @pl.when(t == 0)
    def _():
        u1_ref[...] = jnp.zeros(u1_ref.shape, F32)

    def work(uw_ref, ur_ref):
        per_chunk = (tm // rc) // len(dot_chunks)
        row = 0
        for c, size in enumerate(dot_chunks):
            for ri in range(c * per_chunk, (c + 1) * per_chunk):
                rows = pl.ds(ri * rc, rc)
                cos = jnp.where(use_rope, cos_ref[rows, :], 1.0)
                sin = jnp.where(use_rope, sin_ref[rows, :], 0.0)
                for ci in range(tn // HEAD_DIM):
                    cols = slice(ci * HEAD_DIM, (ci + 1) * HEAD_DIM)
                    x = ur_ref[rows, cols]
                    rot = jnp.where(first_half, pltpu.roll(x, HEAD_DIM - ROPE_HALF // 2, 1),
                                    pltpu.roll(x, ROPE_HALF // 2, 1))
                    o_ref[rows, cols] = ((x * cos + rot * sin) * qs).astype(o_ref.dtype)
            rows = pl.ds(row, size)
            uw_ref[rows, :] = jnp.dot(a_ref[rows, :], w_ref[...], preferred_element_type=F32)
            row += size

    @pl.when(t % 2 == 0)
    def _():
        work(u0_ref, u1_ref)

    @pl.when(t % 2 == 1)
    def _():
        work(u1_ref, u0_ref)


def _qkv_call(h, w, layer, cos_tab, sin_tab, n_rows):
    tn = 1024
    n = 3 * D_MODEL
    nj = n // tn
    n_steps = (n_rows // TM) * nj
    kern = functools.partial(
        _qkv_kernel, n_col_tiles=nj, dot_chunks=(256, 256, 256, 256), rc=64,
        n_rope_blocks=2 * D_MODEL // tn, n_q_blocks=D_MODEL // tn, q_scale=HEAD_DIM ** -0.5 * LOG2_E)
    vmem = (2 * (TM * D_MODEL * 2 + D_MODEL * tn * 2 + TM * tn * 2 + 2 * TM * HEAD_DIM * 4)
            + 2 * TM * tn * 4 + (12 << 20))

    def mm(t):
        tt = jnp.minimum(t, n_steps - 1)
        return tt // nj, tt % nj

    def ep(t):
        te = jnp.maximum(t - 1, 0)
        return te // nj, te % nj

    return pl.pallas_call(
        kern,
        grid=(n_steps + 1,),
        in_specs=[
            pl.BlockSpec((TM, D_MODEL), lambda t: (mm(t)[0], 0)),
            pl.BlockSpec((None, D_MODEL, tn), lambda t: (layer, 0, mm(t)[1])),
            pl.BlockSpec((TM, HEAD_DIM), lambda t: (ep(t)[0], 0)),
            pl.BlockSpec((TM, HEAD_DIM), lambda t: (ep(t)[0], 0)),
        ],
        out_specs=pl.BlockSpec((TM, tn), lambda t: ep(t)),
        out_shape=jax.ShapeDtypeStruct((n_rows, n), BF16),
        scratch_shapes=[pltpu.VMEM((TM, tn), F32), pltpu.VMEM((TM, tn), F32)],
        compiler_params=_params(("arbitrary",), vmem),
        name="qkv_rope",
    )(h, w, cos_tab, sin_tab)


def _attn_kernel(lam_ref, g_ref, q_ref, *refs, tq, tk, n_ctx_keys, n_lat_keys, cast_blocks, lam_init):
    refs = list(refs)
    n_casts = len(cast_blocks)
    kc_ref, vc_ref = refs[:2]
    del refs[:2]
    if n_lat_keys:
        kl_ref, vl_ref = refs[:2]
        del refs[:2]
    cast_in = refs[:n_casts]
    o_ref = refs[n_casts]
    cast_out = refs[n_casts + 1:2 * n_casts + 1]
    m_ref, l_ref, acc_ref = refs[2 * n_casts + 1:]
    step = (pl.program_id(0) * pl.num_programs(1) + pl.program_id(1)) * pl.num_programs(2) + pl.program_id(2)
    for src, dst, n_blocks in zip(cast_in, cast_out, cast_blocks):
        @pl.when(step < n_blocks)
        def _(src=src, dst=dst):
            dst[...] = src[...].astype(dst.dtype)
    m_ref[...] = jnp.full(m_ref.shape, -jnp.inf, F32)
    l_ref[...] = jnp.zeros(l_ref.shape, F32)
    acc_ref[...] = jnp.zeros(acc_ref.shape, F32)

    def scores(k):
        out = []
        for m in range(2):
            cols = slice(m * HEAD_DIM, (m + 1) * HEAD_DIM)
            out.append(lax.dot_general(q_ref[:, cols], k[:, cols], (((1,), (1,)), ((), ())),
                                       preferred_element_type=F32))
        return out

    def softmax_pv(s_maps, v):
        for m, s in enumerate(s_maps):
            n = s.shape[1]
            m_prev = m_ref[m]
            m_next = jnp.maximum(m_prev, jnp.max(s, axis=1, keepdims=True))
            alpha = jnp.exp2(m_prev - m_next)
            p = jnp.exp2(s - jnp.concatenate([m_next] * (n // LANES), axis=1))
            l_ref[m] = alpha * l_ref[m] + jnp.sum(p, axis=1, keepdims=True)
            m_ref[m] = m_next
            pv = jnp.dot(p.astype(BF16), v, preferred_element_type=F32)
            acc_ref[m] = acc_ref[m] * jnp.concatenate([alpha] * (V_HEAD_DIM // LANES), axis=1) + pv

    chunks = [(kc_ref, vc_ref, pl.ds(0, n_ctx_keys))]
    chunks += [(kl_ref, vl_ref, pl.ds(c * tk, tk)) for c in range(n_lat_keys // tk)]
    s_next = scores(chunks[0][0][chunks[0][2], :])
    for c, (_, v_ref, rows) in enumerate(chunks):
        s_cur = s_next
        if c + 1 < len(chunks):
            k_next, _, rows_next = chunks[c + 1]
            s_next = scores(k_next[rows_next, :])
        softmax_pv(s_cur, v_ref[rows, :])

    lp = lam_ref[...]
    lam = (jnp.exp(jnp.sum(lp[0:1] * lp[1:2], axis=1, keepdims=True))
           - jnp.exp(jnp.sum(lp[2:3] * lp[3:4], axis=1, keepdims=True)) + lam_init)
    rep = V_HEAD_DIM // LANES
    o1 = acc_ref[0] / jnp.concatenate([l_ref[0]] * rep, axis=1)
    o2 = acc_ref[1] / jnp.concatenate([l_ref[1]] * rep, axis=1)
    o = o1 - lam * o2
    ms = jnp.mean(o * o, axis=-1, keepdims=True)
    y = o * lax.rsqrt(ms + SUBLN_EPS) * g_ref[...]
    o_ref[...] = (y * (1.0 - lam_init)).astype(o_ref.dtype)


def _attn_call(qkv, lam_p, subln_g, lam_init, latent, casts=()):
    hq = D_MODEL // V_HEAD_DIM
    ctx_row0 = N_LAT // CTX_LEN
    if latent:
        tq, tk, n_lat_keys, n_out = 512, 1024, SEQ, N_LAT
        n_q = SEQ // tq
        q_map = lambda b, h, qi: (b * (SEQ // tq) + qi, h)
    else:
        tq, tk, n_q, n_lat_keys, n_out = CTX_LEN, 512, 1, 0, N_CTX
        q_map = lambda b, h, qi: (ctx_row0 + b, h)
    in_specs = [
        pl.BlockSpec((4, HEAD_DIM), lambda b, h, qi: (0, 0)),
        pl.BlockSpec((1, V_HEAD_DIM), lambda b, h, qi: (0, 0)),
        pl.BlockSpec((tq, V_HEAD_DIM), q_map),
        pl.BlockSpec((CTX_LEN, V_HEAD_DIM), lambda b, h, qi: (ctx_row0 + b, hq + h)),
        pl.BlockSpec((CTX_LEN, V_HEAD_DIM), lambda b, h, qi: (ctx_row0 + b, 2 * hq + h)),
    ]
    args = [lam_p, subln_g.reshape(1, V_HEAD_DIM), qkv, qkv, qkv]
    if latent:
        in_specs += [
            pl.BlockSpec((SEQ, V_HEAD_DIM), lambda b, h, qi: (b, hq + h)),
            pl.BlockSpec((SEQ, V_HEAD_DIM), lambda b, h, qi: (b, 2 * hq + h)),
        ]
        args += [qkv, qkv]
        out_map = lambda b, h, qi: (b * (SEQ // tq) + qi, h)
    else:
        out_map = lambda b, h, qi: (b, h)
    n_steps = BATCH * N_HEADS * n_q
    out_specs = [pl.BlockSpec((tq, V_HEAD_DIM), out_map)]
    out_shape = [jax.ShapeDtypeStruct((n_out, D_MODEL), BF16)]
    cast_shapes, cast_blocks = [], []
    for w, layer in casts:
        rows, cols = math.prod(w.shape[1:-1]), w.shape[-1]
        n_blocks = rows // BF16_SUBLANES
        assert rows % BF16_SUBLANES == 0 and n_blocks <= n_steps

        def block(b, h, qi, n_blocks=n_blocks):
            return jnp.minimum((b * N_HEADS + h) * n_q + qi, n_blocks - 1)

        in_specs.append(pl.BlockSpec((None, BF16_SUBLANES, cols),
                                     lambda b, h, qi, layer=layer, block=block: (layer, block(b, h, qi), 0)))
        args.append(w.reshape(w.shape[0], rows, cols))
        out_specs.append(pl.BlockSpec((BF16_SUBLANES, cols), lambda b, h, qi, block=block: (block(b, h, qi), 0)))
        out_shape.append(jax.ShapeDtypeStruct((rows, cols), BF16))
        cast_shapes.append((1,) + w.shape[1:])
        cast_blocks.append(n_blocks)
    kern = functools.partial(_attn_kernel, tq=tq, tk=tk, n_ctx_keys=CTX_LEN, n_lat_keys=n_lat_keys,
                             cast_blocks=tuple(cast_blocks), lam_init=lam_init)
    outs = pl.pallas_call(
        kern,
        grid=(BATCH, N_HEADS, n_q),
        in_specs=in_specs,
        out_specs=out_specs,
        out_shape=out_shape,
        scratch_shapes=[
            pltpu.VMEM((2, tq, LANES), F32),
            pltpu.VMEM((2, tq, LANES), F32),
            pltpu.VMEM((2, tq, V_HEAD_DIM), F32),
        ],
        compiler_params=_params(("arbitrary", "arbitrary", "arbitrary"), 48 << 20),
        name="diff_attn_lat" if latent else "diff_attn_ctx",
    )(*args)
    return outs[0], [o.reshape(s) for o, s in zip(outs[1:], cast_shapes)]


def _mm_res_kernel(a_ref, w_ref, x_ref, gate_ref, cs_ref, o_ref, *, chunk):
    tm = o_ref.shape[0]
    for c in range(tm // chunk):
        rows = pl.ds(c * chunk, chunk)
        acc = jnp.dot(a_ref[rows, :], w_ref[...], preferred_element_type=F32)
        o_ref[rows, :] = x_ref[rows, :] + gate_ref[...] * (acc * cs_ref[...])


def _mm_res_call(a, w, w_idx, xs, modr, col_scale, layer, gate_chunk, row0, *, tm, tn, grouped=False):
    k = w.shape[-2]
    assert a.shape[0] % tm == 0 and row0 % tm == 0
    n_row_tiles = a.shape[0] // tm
    row_tile0 = row0 // tm
    gate = _mod_row(layer, gate_chunk, tm)
    if grouped:
        assert tn == w.shape[-1]
        a_spec = pl.BlockSpec((tm, k), lambda i, j: (i, j))
        w_spec = pl.BlockSpec((None, None, k, tn), lambda i, j: (w_idx, j, 0, 0))
    else:
        a_spec = pl.BlockSpec((tm, k), lambda i, j: (i, 0))
        w_spec = pl.BlockSpec((None, k, tn), lambda i, j: (w_idx, 0, j))
    x_spec = pl.BlockSpec((tm, tn), lambda i, j: (row_tile0 + i, j))
    vmem = 2 * (tm * k * 2 + k * tn * 2 + 2 * tm * tn * 4) + (12 << 20)
    return pl.pallas_call(
        functools.partial(_mm_res_kernel, chunk=min(tm, 512)),
        grid=(n_row_tiles, D_MODEL // tn),
        in_specs=[
            a_spec,
            w_spec,
            x_spec,
            pl.BlockSpec((None, 1, tn), lambda i, j: (gate(row_tile0 + i), 0, j)),
            pl.BlockSpec((1, tn), lambda i, j: (0, j)),
        ],
        out_specs=x_spec,
        out_shape=jax.ShapeDtypeStruct(xs.shape, F32),
        input_output_aliases={2: 0},
        compiler_params=_params(("arbitrary", "arbitrary"), vmem),
        name="mm_residual",
    )(a, w, xs, modr, col_scale)


def _pool_kernel(x_ref, xp_ref, xn_ref, g_ref, sh_ref, sc_ref, o_ref, hs_ref, *, r, halo, n_lat_tiles):
    i = pl.program_id(0)
    seq_len = jnp.where(i < n_lat_tiles, SEQ, CTX_LEN)
    pos0 = (i * r) & (seq_len - 1)
    is_first = pos0 == 0
    is_last = pos0 + r == seq_len
    g, sh, sc = g_ref[...], sh_ref[...], sc_ref[...]
    hs_ref[0:halo, :] = jnp.where(is_first, 0.0, _norm_mod(xp_ref[...], g, sh, sc))
    hs_ref[halo:halo + r, :] = _norm_mod(x_ref[...], g, sh, sc)
    hs_ref[halo + r:, :] = jnp.where(is_last, 0.0, _norm_mod(xn_ref[...], g, sh, sc))

    rc, cc = 128, 256
    for grp, w in enumerate(POOL_WINDOWS):
        before, after = w // 2, w - 1 - w // 2
        for ri in range(r // rc):
            pos = pos0 + ri * rc + lax.broadcasted_iota(jnp.int32, (rc, 1), 0)
            lo = jnp.maximum(pos - before, 0)
            hi = jnp.minimum(pos + after, seq_len - 1)
            inv_cnt = 1.0 / (hi - lo + 1).astype(F32)
            for ci in range(POOL_GROUP // cc):
                cols = slice(grp * POOL_GROUP + ci * cc, grp * POOL_GROUP + (ci + 1) * cc)
                acc = hs_ref[pl.ds(halo + ri * rc - before, rc), cols]
                for s in range(-before + 1, after + 1):
                    acc = acc + hs_ref[pl.ds(halo + ri * rc + s, rc), cols]
                y = acc * inv_cnt - hs_ref[pl.ds(halo + ri * rc, rc), cols]
                o_ref[pl.ds(ri * rc, rc), cols] = y.astype(o_ref.dtype)


def _pool_call(xs, g, modr, layer, n_rows):
    r, halo = 256, 8
    n_halo_blocks = n_rows // halo
    sh = _mod_row(layer, 0, r)
    sc = _mod_row(layer, 1, r)
    kern = functools.partial(_pool_kernel, r=r, halo=halo, n_lat_tiles=N_LAT // r)
    return pl.pallas_call(
        kern,
        grid=(n_rows // r,),
        in_specs=[
            pl.BlockSpec((r, D_MODEL), lambda i: (i, 0)),
            pl.BlockSpec((halo, D_MODEL), lambda i: (jnp.maximum(i * (r // halo) - 1, 0), 0)),
            pl.BlockSpec((halo, D_MODEL), lambda i: (jnp.minimum((i + 1) * (r // halo), n_halo_blocks - 1), 0)),
            pl.BlockSpec((1, D_MODEL), lambda i: (0, 0)),
            pl.BlockSpec((None, 1, D_MODEL), lambda i: (sh(i), 0, 0)),
            pl.BlockSpec((None, 1, D_MODEL), lambda i: (sc(i), 0, 0)),
        ],
        out_specs=pl.BlockSpec((r, D_MODEL), lambda i: (i, 0)),
        out_shape=jax.ShapeDtypeStruct((n_rows, D_MODEL), BF16),
        scratch_shapes=[pltpu.VMEM((r + 2 * halo, D_MODEL), F32)],
        compiler_params=_params(("arbitrary",), 32 << 20),
        name="norm_mod_pool",
    )(xs, xs, xs, g.reshape(1, D_MODEL), modr, modr)


def _ffn_in_kernel(a_ref, ap_ref, an_ref, wg_ref, wv_ref, cwg_ref, cwv_ref, cbg_ref, cbv_ref, o_ref,
                   aext_ref, u0_ref, u1_ref, *, halo, n_col_tiles, n_lat_tiles, dot_chunks, rc):
    t = pl.program_id(0)
    tm, half = o_ref.shape
    j = jnp.minimum(t, pl.num_programs(0) - 2) % n_col_tiles

    @pl.when(t == 0)
    def _():
        u1_ref[...] = jnp.zeros(u1_ref.shape, F32)

    @pl.when(j == 0)
    def _():
        aext_ref[0:halo, :] = ap_ref[...]
        aext_ref[halo:halo + tm, :] = a_ref[...]
        aext_ref[halo + tm:, :] = an_ref[...]

    ie = jnp.maximum(t - 1, 0) // n_col_tiles
    seq_len = jnp.where(ie < n_lat_tiles, SEQ, CTX_LEN)
    sub = lax.broadcasted_iota(jnp.int32, (8, 1), 0)

    def conv(ur_ref, cols, r0, cw_ref, cb_ref, ccols):
        up = ur_ref[pl.ds(halo + r0 - 1, rc), cols]
        un = ur_ref[pl.ds(halo + r0 + 1, rc), cols]
        uc = ur_ref[pl.ds(halo + r0, rc), cols]
        if r0 % CTX_LEN == 0:
            pos = (ie * tm + r0 + sub) & (seq_len - 1)
            up = jnp.concatenate([jnp.where(pos != 0, up[0:8], 0.0), up[8:]], axis=0)
        if (r0 + rc) % CTX_LEN == 0:
            pos = (ie * tm + r0 + rc - 8 + sub) & (seq_len - 1)
            un = jnp.concatenate([un[:rc - 8], jnp.where(pos != seq_len - 1, un[rc - 8:], 0.0)], axis=0)
        return (up * cw_ref[0:1, ccols] + uc * cw_ref[1:2, ccols] + un * cw_ref[2:3, ccols]
                + cb_ref[:, ccols])

    def work(uw_ref, ur_ref):
        n_pieces = tm // rc
        per_chunk = n_pieces // len(dot_chunks)
        row = 0
        for c, size in enumerate(dot_chunks):
            for ri in range(c * per_chunk, (c + 1) * per_chunk):
                r0 = ri * rc
                for ci in range(half // LANES):
                    ccols = slice(ci * LANES, (ci + 1) * LANES)
                    gate = conv(ur_ref, ccols, r0, cwg_ref, cbg_ref, ccols)
                    val = conv(ur_ref, slice(half + ci * LANES, half + (ci + 1) * LANES), r0, cwv_ref,
                               cbv_ref, ccols)
                    o_ref[pl.ds(r0, rc), ccols] = (
                        gate * (1.0 / (1.0 + jnp.exp(-gate))) * val).astype(o_ref.dtype)
            rows = pl.ds(row, size)
            a = aext_ref[rows, :]
            uw_ref[rows, 0:half] = jnp.dot(a, wg_ref[...], preferred_element_type=F32)
            uw_ref[rows, half:] = jnp.dot(a, wv_ref[...], preferred_element_type=F32)
            row += size

    @pl.when(t % 2 == 0)
    def _():
        work(u0_ref, u1_ref)

    @pl.when(t % 2 == 1)
    def _():
        work(u1_ref, u0_ref)


def _ffn_in_call(h, w_in, w_idx, conv_w, conv_b, layer, n_rows):
    half, halo = 512, BF16_SUBLANES
    n_halo_blocks = n_rows // halo
    ext = TM + 2 * halo
    nj = D_FF // half
    n_steps = (n_rows // TM) * nj
    kern = functools.partial(_ffn_in_kernel, halo=halo, n_col_tiles=nj, n_lat_tiles=N_LAT // TM,
                             dot_chunks=(528, 528), rc=32)
    vmem = (2 * (TM * D_MODEL * 2 + 2 * D_MODEL * half * 2 + TM * half * 2 + 2 * halo * D_MODEL * 2)
            + ext * D_MODEL * 2 + 2 * ext * 2 * half * 4 + (12 << 20))

    def mm(t):
        tt = jnp.minimum(t, n_steps - 1)
        return tt // nj, tt % nj

    def ep(t):
        te = jnp.maximum(t - 1, 0)
        return te // nj, te % nj

    return pl.pallas_call(
        kern,
        grid=(n_steps + 1,),
        in_specs=[
            pl.BlockSpec((TM, D_MODEL), lambda t: (mm(t)[0], 0)),
            pl.BlockSpec((halo, D_MODEL), lambda t: (jnp.maximum(mm(t)[0] * (TM // halo) - 1, 0), 0)),
            pl.BlockSpec((halo, D_MODEL),
                         lambda t: (jnp.minimum((mm(t)[0] + 1) * (TM // halo), n_halo_blocks - 1), 0)),
            pl.BlockSpec((None, D_MODEL, half), lambda t: (w_idx, 0, mm(t)[1])),
            pl.BlockSpec((None, D_MODEL, half), lambda t: (w_idx, 0, nj + mm(t)[1])),
            pl.BlockSpec((None, 3, half), lambda t: (layer, 0, ep(t)[1])),
            pl.BlockSpec((None, 3, half), lambda t: (layer, 0, nj + ep(t)[1])),
            pl.BlockSpec((None, 1, half), lambda t: (layer, 0, ep(t)[1])),
            pl.BlockSpec((None, 1, half), lambda t: (layer, 0, nj + ep(t)[1])),
        ],
        out_specs=pl.BlockSpec((TM, half), lambda t: ep(t)),
        out_shape=jax.ShapeDtypeStruct((n_rows, D_FF), BF16),
        scratch_shapes=[
            pltpu.VMEM((ext, D_MODEL), BF16),
            pltpu.VMEM((ext, 2 * half), F32),
            pltpu.VMEM((ext, 2 * half), F32),
        ],
        compiler_params=_params(("arbitrary",), vmem),
        name="ffn_in_conv_glu",
    )(h, h, h, w_in, w_in, conv_w, conv_w, conv_b, conv_b)


def _rope_tables():
    rows = SEQ // GRID_W
    row = jnp.repeat(jnp.arange(rows), GRID_W).astype(F32)
    col = jnp.tile(jnp.arange(GRID_W), rows).astype(F32)
    inv = 1.0 / (ROPE_BASE ** (jnp.arange(0, ROPE_HALF, 2, dtype=F32) / ROPE_HALF))

    def tab(pos):
        a = pos[:, None] * inv[None, :]
        a = jnp.concatenate([a, a], axis=-1)
        return jnp.cos(a), jnp.sin(a)

    cr, sr = tab(row)
    cc, sc = tab(col)
    cos = jnp.concatenate([cr, cc], axis=-1)
    sin = jnp.concatenate([sr, sc], axis=-1)
    lane = jnp.arange(HEAD_DIM)
    sin = jnp.where((lane % ROPE_HALF) < ROPE_HALF // 2, -sin, sin)
    cos_all = jnp.concatenate([jnp.tile(cos, (BATCH, 1)), jnp.ones((N_CTX, HEAD_DIM), F32)], axis=0)
    sin_all = jnp.concatenate([jnp.tile(sin, (BATCH, 1)), jnp.zeros((N_CTX, HEAD_DIM), F32)], axis=0)
    return cos_all, sin_all


def _ctx_needed_after(i):
    return any((j % N_MIXERS) == 0 for j in range(i + 1, DEPTH))


def kernel(x, c, ctx, c_ctx, ada_w, ada_b, norm_g, attn_qkv, attn_o, attn_lambda, attn_subln_g,
           pool_w, pool_scale, ffn_w_in, ffn_conv_w, ffn_conv_b, ffn_w_out, final_g):
    assert x.shape == (BATCH, SEQ, D_MODEL) and ctx.shape == (BATCH, CTX_LEN, D_MODEL)
    xs = jnp.concatenate([x.reshape(N_LAT, D_MODEL), ctx.reshape(N_CTX, D_MODEL)], axis=0)
    cond = jnp.concatenate(
        [c, c_ctx[None, :], jnp.zeros((COND_ROWS - BATCH - 1, D_MODEL), F32)], axis=0)
    mod = _ada(cond, ada_w, ada_b)
    modr = mod.reshape(DEPTH, COND_ROWS, 6, D_MODEL).transpose(0, 2, 1, 3).reshape(
        DEPTH * 6 * COND_ROWS, 1, D_MODEL)
    cos_tab, sin_tab = _rope_tables()
    ones_row = jnp.ones((1, D_MODEL), F32)

    conv_b = ffn_conv_b.reshape(DEPTH, 1, 2 * D_FF)
    w16 = {("qkv", 0): attn_qkv[0:1].astype(BF16)}

    for i in range(DEPTH):
        need_ctx = _ctx_needed_after(i)
        n_rows = N_ALL if need_ctx else N_LAT
        if i % N_MIXERS == 0:
            a = i // N_MIXERS
            lam_init = 0.8 - 0.6 * math.exp(-0.3 * i)
            h = _norm_mod_call(xs, norm_g[i, 0], modr, i, 0, N_ALL)
            qkv = _qkv_call(h, w16["qkv", a], 0, cos_tab, sin_tab, N_ALL)
            casts = [("o", a, attn_o), ("in", i, ffn_w_in), ("out", i, ffn_w_out)]
            for nxt in range(i + 1, min(i + N_MIXERS, DEPTH)):
                casts += [("pool", nxt // N_MIXERS, pool_w), ("in", nxt, ffn_w_in), ("out", nxt, ffn_w_out)]
            if i + N_MIXERS < DEPTH:
                casts.append(("qkv", a + 1, attn_qkv))
            o_lat, cast_out = _attn_call(qkv, attn_lambda[a], attn_subln_g[a], lam_init, latent=True,
                                         casts=[(w, idx) for _, idx, w in casts])
            for (name, idx, _), w_bf16 in zip(casts, cast_out):
                w16[name, idx] = w_bf16
            xs = _mm_res_call(o_lat, w16["o", a], 0, xs, modr, ones_row, i, 2, 0, tm=TM, tn=512)
            if need_ctx:
                o_ctx, _ = _attn_call(qkv, attn_lambda[a], attn_subln_g[a], lam_init, latent=False)
                xs = _mm_res_call(o_ctx, w16["o", a], 0, xs, modr, ones_row, i, 2, N_LAT, tm=TM, tn=512)
        else:
            p = i // N_MIXERS
            y = _pool_call(xs, norm_g[i, 0], modr, i, n_rows)
            xs = _mm_res_call(y, w16["pool", p], 0, xs, modr, pool_scale[p].reshape(1, D_MODEL), i, 2, 0,
                              tm=TM, tn=POOL_GROUP, grouped=True)
        h = _norm_mod_call(xs, norm_g[i, 1], modr, i, 3, n_rows)
        act = _ffn_in_call(h, w16["in", i], 0, ffn_conv_w, conv_b, i, n_rows)
        xs = _mm_res_call(act, w16["out", i], 0, xs, modr, ones_row, i, 5, 0, tm=TM, tn=512)
    return _final_norm(xs, final_g).reshape(BATCH, SEQ, D_MODEL)
```

```python
import functools
import math

import jax
import jax.numpy as jnp
from jax import lax
from jax.experimental import pallas as pl
from jax.experimental.pallas import tpu as pltpu

D_MODEL = 4096
BATCH = 4
SEQ = 4096
DEPTH = 4
GRID_W = 64
CTX_LEN = 256
N_MIXERS = 2
HEAD_DIM = 128
V_HEAD_DIM = 2 * HEAD_DIM
N_HEADS = D_MODEL // V_HEAD_DIM
ROPE_HALF = HEAD_DIM // 2
ROPE_BASE = 10000.0
POOL_WINDOWS = (2, 4, 8, 16)
N_POOL_GROUPS = len(POOL_WINDOWS)
POOL_GROUP = D_MODEL // N_POOL_GROUPS
D_FF = 7168
EPS = 1e-6
SUBLN_EPS = 1e-5

N_LAT = BATCH * SEQ
N_CTX = BATCH * CTX_LEN
N_ALL = N_LAT + N_CTX
COND_ROWS = 8

V7X_VMEM_BYTES = 64 * 1024 * 1024
VMEM_CAP_BYTES = V7X_VMEM_BYTES - 6 * 1024 * 1024
LANES = 128
BF16_SUBLANES = 16
LOG2_E = math.log2(math.e)

TM = 1024
BF16 = jnp.bfloat16
F32 = jnp.float32


def _params(semantics, vmem_bytes, flags=None):
    return pltpu.CompilerParams(
        dimension_semantics=semantics,
        vmem_limit_bytes=int(min(VMEM_CAP_BYTES, vmem_bytes)),
        flags=flags,
    )


def _mod_row(layer, chunk, tile_rows):
    n_lat_tiles = N_LAT // tile_rows
    tiles_per_seq = SEQ // tile_rows
    base = (layer * 6 + chunk) * COND_ROWS

    def idx(i):
        return base + jnp.where(i < n_lat_tiles, i // tiles_per_seq, BATCH)

    return idx


def _ada_kernel(cond_ref, w_ref, b_ref, o_ref):
    k = pl.program_id(2)
    c = cond_ref[...]
    c = c * (1.0 / (1.0 + jnp.exp(-c)))
    acc = jnp.dot(c.astype(BF16), w_ref[...].astype(BF16), preferred_element_type=F32)

    @pl.when(k == 0)
    def _():
        o_ref[...] = acc + b_ref[...]

    @pl.when(k != 0)
    def _():
        o_ref[...] += acc


def _ada(cond, ada_w, ada_b):
    tn, tk = 2048, 1024
    n = 6 * D_MODEL
    return pl.pallas_call(
        _ada_kernel,
        grid=(DEPTH, n // tn, D_MODEL // tk),
        in_specs=[
            pl.BlockSpec((COND_ROWS, tk), lambda i, j, k: (0, k)),
            pl.BlockSpec((None, tk, tn), lambda i, j, k: (i, k, j)),
            pl.BlockSpec((None, 1, tn), lambda i, j, k: (i, 0, j)),
        ],
        out_specs=pl.BlockSpec((None, COND_ROWS, tn), lambda i, j, k: (i, 0, j)),
        out_shape=jax.ShapeDtypeStruct((DEPTH, COND_ROWS, n), F32),
        compiler_params=_params(("arbitrary", "arbitrary", "arbitrary"), 3 * tk * tn * 4 + (8 << 20)),
        name="ada_mod",
    )(cond, ada_w, ada_b.reshape(DEPTH, 1, n))


def _norm_mod(x, g, sh, sc):
    ms = jnp.mean(x * x, axis=-1, keepdims=True)
    y = x * lax.rsqrt(ms + EPS) * g
    return y * (1.0 + sc) + sh


def _norm_mod_kernel(x_ref, g_ref, sh_ref, sc_ref, o_ref):
    o_ref[...] = _norm_mod(x_ref[...], g_ref[...], sh_ref[...], sc_ref[...]).astype(o_ref.dtype)


def _norm_mod_call(xs, g, modr, layer, chunk0, n_rows):
    r = 512
    sh = _mod_row(layer, chunk0, r)
    sc = _mod_row(layer, chunk0 + 1, r)
    return pl.pallas_call(
        _norm_mod_kernel,
        grid=(n_rows // r,),
        in_specs=[
            pl.BlockSpec((r, D_MODEL), lambda i: (i, 0)),
            pl.BlockSpec((1, D_MODEL), lambda i: (0, 0)),
            pl.BlockSpec((None, 1, D_MODEL), lambda i: (sh(i), 0, 0)),
            pl.BlockSpec((None, 1, D_MODEL), lambda i: (sc(i), 0, 0)),
        ],
        out_specs=pl.BlockSpec((r, D_MODEL), lambda i: (i, 0)),
        out_shape=jax.ShapeDtypeStruct((n_rows, D_MODEL), BF16),
        compiler_params=_params(("arbitrary",), 6 * r * D_MODEL * 4),
        name="norm_mod",
    )(xs, g.reshape(1, D_MODEL), modr, modr)


def _final_norm_kernel(x_ref, g_ref, o_ref):
    x = x_ref[...]
    ms = jnp.mean(x * x, axis=-1, keepdims=True)
    o_ref[...] = x * lax.rsqrt(ms + EPS) * g_ref[...]


def _final_norm(xs, g):
    r = 512
    return pl.pallas_call(
        _final_norm_kernel,
        grid=(N_LAT // r,),
        in_specs=[
            pl.BlockSpec((r, D_MODEL), lambda i: (i, 0)),
            pl.BlockSpec((1, D_MODEL), lambda i: (0, 0)),
        ],
        out_specs=pl.BlockSpec((r, D_MODEL), lambda i: (i, 0)),
        out_shape=jax.ShapeDtypeStruct((N_LAT, D_MODEL), F32),
        compiler_params=_params(("arbitrary",), 6 * r * D_MODEL * 4),
        name="final_norm",
    )(xs, g.reshape(1, D_MODEL))


def _qkv_kernel(a_ref, w_ref, cos_ref, sin_ref, o_ref, u0_ref, u1_ref, *, n_col_tiles, dot_chunks, rc,
                n_rope_blocks, n_q_blocks, q_scale):
    t = pl.program_id(0)
    tm, tn = o_ref.shape
    je = jnp.maximum(t - 1, 0) % n_col_tiles
    use_rope = je < n_rope_blocks
    qs = jnp.where(je < n_q_blocks, q_scale, 1.0).astype(F32)
    lane = lax.broadcasted_iota(jnp.int32, (rc, HEAD_DIM), 1)
    first_half = (lane % ROPE_HALF) < (ROPE_HALF // 2)

    @pl.when(t == 0)
    def _():
        u1_ref[...] = jnp.zeros(u1_ref.shape, F32)

    def work(uw_ref, ur_ref):
        per_chunk = (tm // rc) // len(dot_chunks)
        row = 0
        for c, size in enumerate(dot_chunks):
            for ri in range(c * per_chunk, (c + 1) * per_chunk):
                rows = pl.ds(ri * rc, rc)
                cos = jnp.where(use_rope, cos_ref[rows, :], 1.0)
                sin = jnp.where(use_rope, sin_ref[rows, :], 0.0)
                for ci in range(tn // HEAD_DIM):
                    cols = slice(ci * HEAD_DIM, (ci + 1) * HEAD_DIM)
                    x = ur_ref[rows, cols]
                    rot = jnp.where(first_half, pltpu.roll(x, HEAD_DIM - ROPE_HALF // 2, 1),
                                    pltpu.roll(x, ROPE_HALF // 2, 1))
                    o_ref[rows, cols] = ((x * cos + rot * sin) * qs).astype(o_ref.dtype)
            rows = pl.ds(row, size)
            uw_ref[rows, :] = jnp.dot(a_ref[rows, :], w_ref[...], preferred_element_type=F32)
            row += size

    @pl.when(t % 2 == 0)
    def _():
        work(u0_ref, u1_ref)

    @pl.when(t % 2 == 1)
    def _():
        work(u1_ref, u0_ref)


def _qkv_call(h, w, layer, cos_tab, sin_tab, n_rows):
    tn = 1024
    n = 3 * D_MODEL
    nj = n // tn
    n_steps = (n_rows // TM) * nj
    kern = functools.partial(
        _qkv_kernel, n_col_tiles=nj, dot_chunks=(256, 256, 256, 256), rc=64,
        n_rope_blocks=2 * D_MODEL // tn, n_q_blocks=D_MODEL // tn, q_scale=HEAD_DIM ** -0.5 * LOG2_E)
    vmem = (2 * (TM * D_MODEL * 2 + D_MODEL * tn * 2 + TM * tn * 2 + 2 * TM * HEAD_DIM * 4)
            + 2 * TM * tn * 4 + (12 << 20))

    def mm(t):
        tt = jnp.minimum(t, n_steps - 1)
        return tt // nj, tt % nj

    def ep(t):
        te = jnp.maximum(t - 1, 0)
        return te // nj, te % nj

    return pl.pallas_call(
        kern,
        grid=(n_steps + 1,),
        in_specs=[
            pl.BlockSpec((TM, D_MODEL), lambda t: (mm(t)[0], 0)),
            pl.BlockSpec((None, D_MODEL, tn), lambda t: (layer, 0, mm(t)[1])),
            pl.BlockSpec((TM, HEAD_DIM), lambda t: (ep(t)[0], 0)),
            pl.BlockSpec((TM, HEAD_DIM), lambda t: (ep(t)[0], 0)),
        ],
        out_specs=pl.BlockSpec((TM, tn), lambda t: ep(t)),
        out_shape=jax.ShapeDtypeStruct((n_rows, n), BF16),
        scratch_shapes=[pltpu.VMEM((TM, tn), F32), pltpu.VMEM((TM, tn), F32)],
        compiler_params=_params(("arbitrary",), vmem),
        name="qkv_rope",
    )(h, w, cos_tab, sin_tab)


def _attn_kernel(lam_ref, g_ref, q_ref, *refs, tq, tk, n_ctx_keys, n_lat_keys, cast_blocks, lam_init):
    refs = list(refs)
    n_casts = len(cast_blocks)
    kc_ref, vc_ref = refs[:2]
    del refs[:2]
    if n_lat_keys:
        kl_ref, vl_ref = refs[:2]
        del refs[:2]
    cast_in = refs[:n_casts]
    o_ref = refs[n_casts]
    cast_out = refs[n_casts + 1:2 * n_casts + 1]
    m_ref, l_ref, acc_ref = refs[2 * n_casts + 1:]
    step = (pl.program_id(0) * pl.num_programs(1) + pl.program_id(1)) * pl.num_programs(2) + pl.program_id(2)
    for src, dst, n_blocks in zip(cast_in, cast_out, cast_blocks):
        @pl.when(step < n_blocks)
        def _(src=src, dst=dst):
            dst[...] = src[...].astype(dst.dtype)
    m_ref[...] = jnp.full(m_ref.shape, -jnp.inf, F32)
    l_ref[...] = jnp.zeros(l_ref.shape, F32)
    acc_ref[...] = jnp.zeros(acc_ref.shape, F32)

    def scores(k):
        out = []
        for m in range(2):
            cols = slice(m * HEAD_DIM, (m + 1) * HEAD_DIM)
            out.append(lax.dot_general(q_ref[:, cols], k[:, cols], (((1,), (1,)), ((), ())),
                                       preferred_element_type=F32))
        return out

    def softmax_pv(s_maps, v):
        for m, s in enumerate(s_maps):
            n = s.shape[1]
            m_prev = m_ref[m]
            m_next = jnp.maximum(m_prev, jnp.max(s, axis=1, keepdims=True))
            alpha = jnp.exp2(m_prev - m_next)
            p = jnp.exp2(s - jnp.concatenate([m_next] * (n // LANES), axis=1))
            l_ref[m] = alpha * l_ref[m] + jnp.sum(p, axis=1, keepdims=True)
            m_ref[m] = m_next
            pv = jnp.dot(p.astype(BF16), v, preferred_element_type=F32)
            acc_ref[m] = acc_ref[m] * jnp.concatenate([alpha] * (V_HEAD_DIM // LANES), axis=1) + pv

    chunks = [(kc_ref, vc_ref, pl.ds(0, n_ctx_keys))]
    chunks += [(kl_ref, vl_ref, pl.ds(c * tk, tk)) for c in range(n_lat_keys // tk)]
    s_next = scores(chunks[0][0][chunks[0][2], :])
    for c, (_, v_ref, rows) in enumerate(chunks):
        s_cur = s_next
        if c + 1 < len(chunks):
            k_next, _, rows_next = chunks[c + 1]
            s_next = scores(k_next[rows_next, :])
        softmax_pv(s_cur, v_ref[rows, :])

    lp = lam_ref[...]
    lam = (jnp.exp(jnp.sum(lp[0:1] * lp[1:2], axis=1, keepdims=True))
           - jnp.exp(jnp.sum(lp[2:3] * lp[3:4], axis=1, keepdims=True)) + lam_init)
    rep = V_HEAD_DIM // LANES
    o1 = acc_ref[0] / jnp.concatenate([l_ref[0]] * rep, axis=1)
    o2 = acc_ref[1] / jnp.concatenate([l_ref[1]] * rep, axis=1)
    o = o1 - lam * o2
    ms = jnp.mean(o * o, axis=-1, keepdims=True)
    y = o * lax.rsqrt(ms + SUBLN_EPS) * g_ref[...]
    o_ref[...] = (y * (1.0 - lam_init)).astype(o_ref.dtype)


def _attn_call(qkv, lam_p, subln_g, lam_init, latent, casts=()):
    hq = D_MODEL // V_HEAD_DIM
    ctx_row0 = N_LAT // CTX_LEN
    if latent:
        tq, tk, n_lat_keys, n_out = 512, 1024, SEQ, N_LAT
        n_q = SEQ // tq
        q_map = lambda b, h, qi: (b * (SEQ // tq) + qi, h)
    else:
        tq, tk, n_q, n_lat_keys, n_out = CTX_LEN, 512, 1, 0, N_CTX
        q_map = lambda b, h, qi: (ctx_row0 + b, h)
    in_specs = [
        pl.BlockSpec((4, HEAD_DIM), lambda b, h, qi: (0, 0)),
        pl.BlockSpec((1, V_HEAD_DIM), lambda b, h, qi: (0, 0)),
        pl.BlockSpec((tq, V_HEAD_DIM), q_map),
        pl.BlockSpec((CTX_LEN, V_HEAD_DIM), lambda b, h, qi: (ctx_row0 + b, hq + h)),
        pl.BlockSpec((CTX_LEN, V_HEAD_DIM), lambda b, h, qi: (ctx_row0 + b, 2 * hq + h)),
    ]
    args = [lam_p, subln_g.reshape(1, V_HEAD_DIM), qkv, qkv, qkv]
    if latent:
        in_specs += [
            pl.BlockSpec((SEQ, V_HEAD_DIM), lambda b, h, qi: (b, hq + h)),
            pl.BlockSpec((SEQ, V_HEAD_DIM), lambda b, h, qi: (b, 2 * hq + h)),
        ]
        args += [qkv, qkv]
        out_map = lambda b, h, qi: (b * (SEQ // tq) + qi, h)
    else:
        out_map = lambda b, h, qi: (b, h)
    n_steps = BATCH * N_HEADS * n_q
    out_specs = [pl.BlockSpec((tq, V_HEAD_DIM), out_map)]
    out_shape = [jax.ShapeDtypeStruct((n_out, D_MODEL), BF16)]
    cast_shapes, cast_blocks = [], []
    for w, layer in casts:
        rows, cols = math.prod(w.shape[1:-1]), w.shape[-1]
        n_blocks = rows // BF16_SUBLANES
        assert rows % BF16_SUBLANES == 0 and n_blocks <= n_steps

        def block(b, h, qi, n_blocks=n_blocks):
            return jnp.minimum((b * N_HEADS + h) * n_q + qi, n_blocks - 1)

        in_specs.append(pl.BlockSpec((None, BF16_SUBLANES, cols),
                                     lambda b, h, qi, layer=layer, block=block: (layer, block(b, h, qi), 0)))
        args.append(w.reshape(w.shape[0], rows, cols))
        out_specs.append(pl.BlockSpec((BF16_SUBLANES, cols), lambda b, h, qi, block=block: (block(b, h, qi), 0)))
        out_shape.append(jax.ShapeDtypeStruct((rows, cols), BF16))
        cast_shapes.append((1,) + w.shape[1:])
        cast_blocks.append(n_blocks)
    kern = functools.partial(_attn_kernel, tq=tq, tk=tk, n_ctx_keys=CTX_LEN, n_lat_keys=n_lat_keys,
                             cast_blocks=tuple(cast_blocks), lam_init=lam_init)
    outs = pl.pallas_call(
        kern,
        grid=(BATCH, N_HEADS, n_q),
        in_specs=in_specs,
        out_specs=out_specs,
        out_shape=out_shape,
        scratch_shapes=[
            pltpu.VMEM((2, tq, LANES), F32),
            pltpu.VMEM((2, tq, LANES), F32),
            pltpu.VMEM((2, tq, V_HEAD_DIM), F32),
        ],
        compiler_params=_params(("arbitrary", "arbitrary", "arbitrary"), 48 << 20),
        name="diff_attn_lat" if latent else "diff_attn_ctx",
    )(*args)
    return outs[0], [o.reshape(s) for o, s in zip(outs[1:], cast_shapes)]


def _mm_res_kernel(a_ref, w_ref, x_ref, gate_ref, cs_ref, o_ref, *, chunk):
    tm = o_ref.shape[0]
    for c in range(tm // chunk):
        rows = pl.ds(c * chunk, chunk)
        acc = jnp.dot(a_ref[rows, :], w_ref[...], preferred_element_type=F32)
        o_ref[rows, :] = x_ref[rows, :] + gate_ref[...] * (acc * cs_ref[...])


def _mm_res_call(a, w, w_idx, xs, modr, col_scale, layer, gate_chunk, row0, *, tm, tn, grouped=False):
    k = w.shape[-2]
    assert a.shape[0] % tm == 0 and row0 % tm == 0
    n_row_tiles = a.shape[0] // tm
    row_tile0 = row0 // tm
    gate = _mod_row(layer, gate_chunk, tm)
    if grouped:
        assert tn == w.shape[-1]
        a_spec = pl.BlockSpec((tm, k), lambda i, j: (i, j))
        w_spec = pl.BlockSpec((None, None, k, tn), lambda i, j: (w_idx, j, 0, 0))
    else:
        a_spec = pl.BlockSpec((tm, k), lambda i, j: (i, 0))
        w_spec = pl.BlockSpec((None, k, tn), lambda i, j: (w_idx, 0, j))
    x_spec = pl.BlockSpec((tm, tn), lambda i, j: (row_tile0 + i, j))
    vmem = 2 * (tm * k * 2 + k * tn * 2 + 2 * tm * tn * 4) + (12 << 20)
    return pl.pallas_call(
        functools.partial(_mm_res_kernel, chunk=min(tm, 512)),
        grid=(n_row_tiles, D_MODEL // tn),
        in_specs=[
            a_spec,
            w_spec,
            x_spec,
            pl.BlockSpec((None, 1, tn), lambda i, j: (gate(row_tile0 + i), 0, j)),
            pl.BlockSpec((1, tn), lambda i, j: (0, j)),
        ],
        out_specs=x_spec,
        out_shape=jax.ShapeDtypeStruct(xs.shape, F32),
        input_output_aliases={2: 0},
        compiler_params=_params(("arbitrary", "arbitrary"), vmem),
        name="mm_residual",
    )(a, w, xs, modr, col_scale)


def _pool_kernel(x_ref, xp_ref, xn_ref, g_ref, sh_ref, sc_ref, o_ref, hs_ref, *, r, halo, n_lat_tiles):
    i = pl.program_id(0)
    seq_len = jnp.where(i < n_lat_tiles, SEQ, CTX_LEN)
    pos0 = (i * r) & (seq_len - 1)
    is_first = pos0 == 0
    is_last = pos0 + r == seq_len
    g, sh, sc = g_ref[...], sh_ref[...], sc_ref[...]
    hs_ref[0:halo, :] = jnp.where(is_first, 0.0, _norm_mod(xp_ref[...], g, sh, sc))
    hs_ref[halo:halo + r, :] = _norm_mod(x_ref[...], g, sh, sc)
    hs_ref[halo + r:, :] = jnp.where(is_last, 0.0, _norm_mod(xn_ref[...], g, sh, sc))

    rc, cc = 128, 256
    for grp, w in enumerate(POOL_WINDOWS):
        before, after = w // 2, w - 1 - w // 2
        for ri in range(r // rc):
            pos = pos0 + ri * rc + lax.broadcasted_iota(jnp.int32, (rc, 1), 0)
            lo = jnp.maximum(pos - before, 0)
            hi = jnp.minimum(pos + after, seq_len - 1)
            inv_cnt = 1.0 / (hi - lo + 1).astype(F32)
            for ci in range(POOL_GROUP // cc):
                cols = slice(grp * POOL_GROUP + ci * cc, grp * POOL_GROUP + (ci + 1) * cc)
                acc = hs_ref[pl.ds(halo + ri * rc - before, rc), cols]
                for s in range(-before + 1, after + 1):
                    acc = acc + hs_ref[pl.ds(halo + ri * rc + s, rc), cols]
                y = acc * inv_cnt - hs_ref[pl.ds(halo + ri * rc, rc), cols]
                o_ref[pl.ds(ri * rc, rc), cols] = y.astype(o_ref.dtype)


def _pool_call(xs, g, modr, layer, n_rows):
    r, halo = 256, 8
    n_halo_blocks = n_rows // halo
    sh = _mod_row(layer, 0, r)
    sc = _mod_row(layer, 1, r)
    kern = functools.partial(_pool_kernel, r=r, halo=halo, n_lat_tiles=N_LAT // r)
    return pl.pallas_call(
        kern,
        grid=(n_rows // r,),
        in_specs=[
            pl.BlockSpec((r, D_MODEL), lambda i: (i, 0)),
            pl.BlockSpec((halo, D_MODEL), lambda i: (jnp.maximum(i * (r // halo) - 1, 0), 0)),
            pl.BlockSpec((halo, D_MODEL), lambda i: (jnp.minimum((i + 1) * (r // halo), n_halo_blocks - 1), 0)),
            pl.BlockSpec((1, D_MODEL), lambda i: (0, 0)),
            pl.BlockSpec((None, 1, D_MODEL), lambda i: (sh(i), 0, 0)),
            pl.BlockSpec((None, 1, D_MODEL), lambda i: (sc(i), 0, 0)),
        ],
        out_specs=pl.BlockSpec((r, D_MODEL), lambda i: (i, 0)),
        out_shape=jax.ShapeDtypeStruct((n_rows, D_MODEL), BF16),
        scratch_shapes=[pltpu.VMEM((r + 2 * halo, D_MODEL), F32)],
        compiler_params=_params(("arbitrary",), 32 << 20),
        name="norm_mod_pool",
    )(xs, xs, xs, g.reshape(1, D_MODEL), modr, modr)


def _ffn_in_kernel(a_ref, ap_ref, an_ref, wg_ref, wv_ref, cwg_ref, cwv_ref, cbg_ref, cbv_ref, o_ref,
                   aext_ref, u_ref, *, halo, n_lat_tiles, dot_chunks, rc):
    i = pl.program_id(0)
    j = pl.program_id(1)
    tm, half = o_ref.shape

    @pl.when(j == 0)
    def _():
        aext_ref[0:halo, :] = ap_ref[...]
        aext_ref[halo:halo + tm, :] = a_ref[...]
        aext_ref[halo + tm:, :] = an_ref[...]

    row = 0
    for size in dot_chunks:
        rows = pl.ds(row, size)
        a = aext_ref[rows, :]
        u_ref[rows, 0:half] = jnp.dot(a, wg_ref[...], preferred_element_type=F32)
        u_ref[rows, half:] = jnp.dot(a, wv_ref[...], preferred_element_type=F32)
        row += size

    seq_len = jnp.where(i < n_lat_tiles, SEQ, CTX_LEN)
    cw = jnp.concatenate([cwg_ref[...], cwv_ref[...]], axis=1)
    cb = jnp.concatenate([cbg_ref[...], cbv_ref[...]], axis=1)
    sub = lax.broadcasted_iota(jnp.int32, (8, 1), 0)
    for ri in range(tm // rc):
        r0 = ri * rc
        up = u_ref[pl.ds(halo + r0 - 1, rc), :]
        un = u_ref[pl.ds(halo + r0 + 1, rc), :]
        uc = u_ref[pl.ds(halo + r0, rc), :]
        if r0 % CTX_LEN == 0:
            pos = (i * tm + r0 + sub) & (seq_len - 1)
            up = jnp.concatenate([jnp.where(pos != 0, up[0:8], 0.0), up[8:]], axis=0)
        if (r0 + rc) % CTX_LEN == 0:
            pos = (i * tm + r0 + rc - 8 + sub) & (seq_len - 1)
            un = jnp.concatenate([un[:rc - 8], jnp.where(pos != seq_len - 1, un[rc - 8:], 0.0)], axis=0)
        conv = up * cw[0:1] + uc * cw[1:2] + un * cw[2:3] + cb
        gate = conv[:, :half]
        val = conv[:, half:]
        o_ref[pl.ds(r0, rc), :] = (gate * (1.0 / (1.0 + jnp.exp(-gate))) * val).astype(o_ref.dtype)


def _ffn_in_call(h, w_in, w_idx, conv_w, conv_b, layer, n_rows):
    half, halo = 512, BF16_SUBLANES
    n_halo_blocks = n_rows // halo
    ext = TM + 2 * halo
    nj = D_FF // half
    kern = functools.partial(_ffn_in_kernel, halo=halo, n_lat_tiles=N_LAT // TM,
                             dot_chunks=(272, 272, 256, 256), rc=32)
    vmem = (2 * (TM * D_MODEL * 2 + 2 * D_MODEL * half * 2 + TM * half * 2 + 2 * halo * D_MODEL * 2)
            + ext * D_MODEL * 2 + ext * 2 * half * 4 + (12 << 20))
    return pl.pallas_call(
        kern,
        grid=(n_rows // TM, nj),
        in_specs=[
            pl.BlockSpec((TM, D_MODEL), lambda i, j: (i, 0)),
            pl.BlockSpec((halo, D_MODEL), lambda i, j: (jnp.maximum(i * (TM // halo) - 1, 0), 0)),
            pl.BlockSpec((halo, D_MODEL),
                         lambda i, j: (jnp.minimum((i + 1) * (TM // halo), n_halo_blocks - 1), 0)),
            pl.BlockSpec((None, D_MODEL, half), lambda i, j: (w_idx, 0, j)),
            pl.BlockSpec((None, D_MODEL, half), lambda i, j: (w_idx, 0, nj + j)),
            pl.BlockSpec((None, 3, half), lambda i, j: (layer, 0, j)),
            pl.BlockSpec((None, 3, half), lambda i, j: (layer, 0, nj + j)),
            pl.BlockSpec((None, 1, half), lambda i, j: (layer, 0, j)),
            pl.BlockSpec((None, 1, half), lambda i, j: (layer, 0, nj + j)),
        ],
        out_specs=pl.BlockSpec((TM, half), lambda i, j: (i, j)),
        out_shape=jax.ShapeDtypeStruct((n_rows, D_FF), BF16),
        scratch_shapes=[
            pltpu.VMEM((ext, D_MODEL), BF16),
            pltpu.VMEM((ext, 2 * half), F32),
        ],
        compiler_params=_params(("arbitrary", "arbitrary"), vmem),
        name="ffn_in_conv_glu",
    )(h, h, h, w_in, w_in, conv_w, conv_w, conv_b, conv_b)


def _rope_tables():
    rows = SEQ // GRID_W
    row = jnp.repeat(jnp.arange(rows), GRID_W).astype(F32)
    col = jnp.tile(jnp.arange(GRID_W), rows).astype(F32)
    inv = 1.0 / (ROPE_BASE ** (jnp.arange(0, ROPE_HALF, 2, dtype=F32) / ROPE_HALF))

    def tab(pos):
        a = pos[:, None] * inv[None, :]
        a = jnp.concatenate([a, a], axis=-1)
        return jnp.cos(a), jnp.sin(a)

    cr, sr = tab(row)
    cc, sc = tab(col)
    cos = jnp.concatenate([cr, cc], axis=-1)
    sin = jnp.concatenate([sr, sc], axis=-1)
    lane = jnp.arange(HEAD_DIM)
    sin = jnp.where((lane % ROPE_HALF) < ROPE_HALF // 2, -sin, sin)
    cos_all = jnp.concatenate([jnp.tile(cos, (BATCH, 1)), jnp.ones((N_CTX, HEAD_DIM), F32)], axis=0)
    sin_all = jnp.concatenate([jnp.tile(sin, (BATCH, 1)), jnp.zeros((N_CTX, HEAD_DIM), F32)], axis=0)
    return cos_all, sin_all


def _ctx_needed_after(i):
    return any((j % N_MIXERS) == 0 for j in range(i + 1, DEPTH))


def kernel(x, c, ctx, c_ctx, ada_w, ada_b, norm_g, attn_qkv, attn_o, attn_lambda, attn_subln_g,
           pool_w, pool_scale, ffn_w_in, ffn_conv_w, ffn_conv_b, ffn_w_out, final_g):
    assert x.shape == (BATCH, SEQ, D_MODEL) and ctx.shape == (BATCH, CTX_LEN, D_MODEL)
    xs = jnp.concatenate([x.reshape(N_LAT, D_MODEL), ctx.reshape(N_CTX, D_MODEL)], axis=0)
    cond = jnp.concatenate(
        [c, c_ctx[None, :], jnp.zeros((COND_ROWS - BATCH - 1, D_MODEL), F32)], axis=0)
    mod = _ada(cond, ada_w, ada_b)
    modr = mod.reshape(DEPTH, COND_ROWS, 6, D_MODEL).transpose(0, 2, 1, 3).reshape(
        DEPTH * 6 * COND_ROWS, 1, D_MODEL)
    cos_tab, sin_tab = _rope_tables()
    ones_row = jnp.ones((1, D_MODEL), F32)

    conv_b = ffn_conv_b.reshape(DEPTH, 1, 2 * D_FF)
    w16 = {("qkv", 0): attn_qkv[0:1].astype(BF16)}

    for i in range(DEPTH):
        need_ctx = _ctx_needed_after(i)
        n_rows = N_ALL if need_ctx else N_LAT
        if i % N_MIXERS == 0:
            a = i // N_MIXERS
            lam_init = 0.8 - 0.6 * math.exp(-0.3 * i)
            h = _norm_mod_call(xs, norm_g[i, 0], modr, i, 0, N_ALL)
            qkv = _qkv_call(h, w16["qkv", a], 0, cos_tab, sin_tab, N_ALL)
            casts = [("o", a, attn_o), ("in", i, ffn_w_in), ("out", i, ffn_w_out)]
            for nxt in range(i + 1, min(i + N_MIXERS, DEPTH)):
                casts += [("pool", nxt // N_MIXERS, pool_w), ("in", nxt, ffn_w_in), ("out", nxt, ffn_w_out)]
            if i + N_MIXERS < DEPTH:
                casts.append(("qkv", a + 1, attn_qkv))
            o_lat, cast_out = _attn_call(qkv, attn_lambda[a], attn_subln_g[a], lam_init, latent=True,
                                         casts=[(w, idx) for _, idx, w in casts])
            for (name, idx, _), w_bf16 in zip(casts, cast_out):
                w16[name, idx] = w_bf16
            xs = _mm_res_call(o_lat, w16["o", a], 0, xs, modr, ones_row, i, 2, 0, tm=TM, tn=512)
            if need_ctx:
                o_ctx, _ = _attn_call(qkv, attn_lambda[a], attn_subln_g[a], lam_init, latent=False)
                xs = _mm_res_call(o_ctx, w16["o", a], 0, xs, modr, ones_row, i, 2, N_LAT, tm=TM, tn=512)
        else:
            p = i // N_MIXERS
            y = _pool_call(xs, norm_g[i, 0], modr, i, n_rows)
            xs = _mm_res_call(y, w16["pool", p], 0, xs, modr, pool_scale[p].reshape(1, D_MODEL), i, 2, 0,
                              tm=TM, tn=POOL_GROUP, grouped=True)
        h = _norm_mod_call(xs, norm_g[i, 1], modr, i, 3, n_rows)
        act = _ffn_in_call(h, w16["in", i], 0, ffn_conv_w, conv_b, i, n_rows)
        xs = _mm_res_call(act, w16["out", i], 0, xs, modr, ones_row, i, 5, 0, tm=TM, tn=512)
    return _final_norm(xs, final_g).reshape(BATCH, SEQ, D_MODEL)
```

```python
import functools
import math

import jax
import jax.numpy as jnp
from jax import lax
from jax.experimental import pallas as pl
from jax.experimental.pallas import tpu as pltpu

D_MODEL = 4096
BATCH = 4
SEQ = 4096
DEPTH = 4
GRID_W = 64
CTX_LEN = 256
N_MIXERS = 2
HEAD_DIM = 128
V_HEAD_DIM = 2 * HEAD_DIM
N_HEADS = D_MODEL // V_HEAD_DIM
ROPE_HALF = HEAD_DIM // 2
ROPE_BASE = 10000.0
POOL_WINDOWS = (2, 4, 8, 16)
N_POOL_GROUPS = len(POOL_WINDOWS)
POOL_GROUP = D_MODEL // N_POOL_GROUPS
D_FF = 7168
EPS = 1e-6
SUBLN_EPS = 1e-5

N_LAT = BATCH * SEQ
N_CTX = BATCH * CTX_LEN
N_ALL = N_LAT + N_CTX
COND_ROWS = 8

V7X_VMEM_BYTES = 64 * 1024 * 1024
VMEM_CAP_BYTES = V7X_VMEM_BYTES - 6 * 1024 * 1024
LANES = 128
BF16_SUBLANES = 16
LOG2_E = math.log2(math.e)

TM = 1024
BF16 = jnp.bfloat16
F32 = jnp.float32


def _params(semantics, vmem_bytes, flags=None):
    return pltpu.CompilerParams(
        dimension_semantics=semantics,
        vmem_limit_bytes=int(min(VMEM_CAP_BYTES, vmem_bytes)),
        flags=flags,
    )


def _mod_row(layer, chunk, tile_rows):
    n_lat_tiles = N_LAT // tile_rows
    tiles_per_seq = SEQ // tile_rows
    base = (layer * 6 + chunk) * COND_ROWS

    def idx(i):
        return base + jnp.where(i < n_lat_tiles, i // tiles_per_seq, BATCH)

    return idx


def _ada_kernel(cond_ref, w_ref, b_ref, o_ref):
    k = pl.program_id(2)
    c = cond_ref[...]
    c = c * (1.0 / (1.0 + jnp.exp(-c)))
    acc = jnp.dot(c.astype(BF16), w_ref[...].astype(BF16), preferred_element_type=F32)

    @pl.when(k == 0)
    def _():
        o_ref[...] = acc + b_ref[...]

    @pl.when(k != 0)
    def _():
        o_ref[...] += acc


def _ada(cond, ada_w, ada_b):
    tn, tk = 2048, 1024
    n = 6 * D_MODEL
    return pl.pallas_call(
        _ada_kernel,
        grid=(DEPTH, n // tn, D_MODEL // tk),
        in_specs=[
            pl.BlockSpec((COND_ROWS, tk), lambda i, j, k: (0, k)),
            pl.BlockSpec((None, tk, tn), lambda i, j, k: (i, k, j)),
            pl.BlockSpec((None, 1, tn), lambda i, j, k: (i, 0, j)),
        ],
        out_specs=pl.BlockSpec((None, COND_ROWS, tn), lambda i, j, k: (i, 0, j)),
        out_shape=jax.ShapeDtypeStruct((DEPTH, COND_ROWS, n), F32),
        compiler_params=_params(("arbitrary", "arbitrary", "arbitrary"), 3 * tk * tn * 4 + (8 << 20)),
        name="ada_mod",
    )(cond, ada_w, ada_b.reshape(DEPTH, 1, n))


def _norm_mod(x, g, sh, sc):
    ms = jnp.mean(x * x, axis=-1, keepdims=True)
    y = x * lax.rsqrt(ms + EPS) * g
    return y * (1.0 + sc) + sh


def _norm_mod_kernel(x_ref, g_ref, sh_ref, sc_ref, o_ref):
    o_ref[...] = _norm_mod(x_ref[...], g_ref[...], sh_ref[...], sc_ref[...]).astype(o_ref.dtype)


def _norm_mod_call(xs, g, modr, layer, chunk0, n_rows):
    r = 512
    sh = _mod_row(layer, chunk0, r)
    sc = _mod_row(layer, chunk0 + 1, r)
    return pl.pallas_call(
        _norm_mod_kernel,
        grid=(n_rows // r,),
        in_specs=[
            pl.BlockSpec((r, D_MODEL), lambda i: (i, 0)),
            pl.BlockSpec((1, D_MODEL), lambda i: (0, 0)),
            pl.BlockSpec((None, 1, D_MODEL), lambda i: (sh(i), 0, 0)),
            pl.BlockSpec((None, 1, D_MODEL), lambda i: (sc(i), 0, 0)),
        ],
        out_specs=pl.BlockSpec((r, D_MODEL), lambda i: (i, 0)),
        out_shape=jax.ShapeDtypeStruct((n_rows, D_MODEL), BF16),
        compiler_params=_params(("arbitrary",), 6 * r * D_MODEL * 4),
        name="norm_mod",
    )(xs, g.reshape(1, D_MODEL), modr, modr)


def _final_norm_kernel(x_ref, g_ref, o_ref):
    x = x_ref[...]
    ms = jnp.mean(x * x, axis=-1, keepdims=True)
    o_ref[...] = x * lax.rsqrt(ms + EPS) * g_ref[...]


def _final_norm(xs, g):
    r = 512
    return pl.pallas_call(
        _final_norm_kernel,
        grid=(N_LAT // r,),
        in_specs=[
            pl.BlockSpec((r, D_MODEL), lambda i: (i, 0)),
            pl.BlockSpec((1, D_MODEL), lambda i: (0, 0)),
        ],
        out_specs=pl.BlockSpec((r, D_MODEL), lambda i: (i, 0)),
        out_shape=jax.ShapeDtypeStruct((N_LAT, D_MODEL), F32),
        compiler_params=_params(("arbitrary",), 6 * r * D_MODEL * 4),
        name="final_norm",
    )(xs, g.reshape(1, D_MODEL))


def _qkv_kernel(a_ref, w_ref, cos_ref, sin_ref, o_ref, u0_ref, u1_ref, *, n_col_tiles, dot_chunks, rc,
                n_rope_blocks, n_q_blocks, q_scale):
    t = pl.program_id(0)
    tm, tn = o_ref.shape
    je = jnp.maximum(t - 1, 0) % n_col_tiles
    use_rope = je < n_rope_blocks
    qs = jnp.where(je < n_q_blocks, q_scale, 1.0).astype(F32)
    lane = lax.broadcasted_iota(jnp.int32, (rc, HEAD_DIM), 1)
    first_half = (lane % ROPE_HALF) < (ROPE_HALF // 2)

    @pl.when(t == 0)
    def _():
        u1_ref[...] = jnp.zeros(u1_ref.shape, F32)

    def work(uw_ref, ur_ref):
        per_chunk = (tm // rc) // len(dot_chunks)
        row = 0
        for c, size in enumerate(dot_chunks):
            for ri in range(c * per_chunk, (c + 1) * per_chunk):
                rows = pl.ds(ri * rc, rc)
                cos = jnp.where(use_rope, cos_ref[rows, :], 1.0)
                sin = jnp.where(use_rope, sin_ref[rows, :], 0.0)
                for ci in range(tn // HEAD_DIM):
                    cols = slice(ci * HEAD_DIM, (ci + 1) * HEAD_DIM)
                    x = ur_ref[rows, cols]
                    rot = jnp.where(first_half, pltpu.roll(x, HEAD_DIM - ROPE_HALF // 2, 1),
                                    pltpu.roll(x, ROPE_HALF // 2, 1))
                    o_ref[rows, cols] = ((x * cos + rot * sin) * qs).astype(o_ref.dtype)
            rows = pl.ds(row, size)
            uw_ref[rows, :] = jnp.dot(a_ref[rows, :], w_ref[...], preferred_element_type=F32)
            row += size

    @pl.when(t % 2 == 0)
    def _():
        work(u0_ref, u1_ref)

    @pl.when(t % 2 == 1)
    def _():
        work(u1_ref, u0_ref)


def _qkv_call(h, w, layer, cos_tab, sin_tab, n_rows):
    tn = 1024
    n = 3 * D_MODEL
    nj = n // tn
    n_steps = (n_rows // TM) * nj
    kern = functools.partial(
        _qkv_kernel, n_col_tiles=nj, dot_chunks=(256, 256, 256, 256), rc=64,
        n_rope_blocks=2 * D_MODEL // tn, n_q_blocks=D_MODEL // tn, q_scale=HEAD_DIM ** -0.5 * LOG2_E)
    vmem = (2 * (TM * D_MODEL * 2 + D_MODEL * tn * 2 + TM * tn * 2 + 2 * TM * HEAD_DIM * 4)
            + 2 * TM * tn * 4 + (12 << 20))

    def mm(t):
        tt = jnp.minimum(t, n_steps - 1)
        return tt // nj, tt % nj

    def ep(t):
        te = jnp.maximum(t - 1, 0)
        return te // nj, te % nj

    return pl.pallas_call(
        kern,
        grid=(n_steps + 1,),
        in_specs=[
            pl.BlockSpec((TM, D_MODEL), lambda t: (mm(t)[0], 0)),
            pl.BlockSpec((None, D_MODEL, tn), lambda t: (layer, 0, mm(t)[1])),
            pl.BlockSpec((TM, HEAD_DIM), lambda t: (ep(t)[0], 0)),
            pl.BlockSpec((TM, HEAD_DIM), lambda t: (ep(t)[0], 0)),
        ],
        out_specs=pl.BlockSpec((TM, tn), lambda t: ep(t)),
        out_shape=jax.ShapeDtypeStruct((n_rows, n), BF16),
        scratch_shapes=[pltpu.VMEM((TM, tn), F32), pltpu.VMEM((TM, tn), F32)],
        compiler_params=_params(("arbitrary",), vmem),
        name="qkv_rope",
    )(h, w, cos_tab, sin_tab)


def _attn_kernel(lam_ref, g_ref, q_ref, *refs, tq, tk, n_ctx_keys, n_lat_keys, cast_blocks, lam_init):
    refs = list(refs)
    n_casts = len(cast_blocks)
    kc_ref, vc_ref = refs[:2]
    del refs[:2]
    if n_lat_keys:
        kl_ref, vl_ref = refs[:2]
        del refs[:2]
    cast_in = refs[:n_casts]
    o_ref = refs[n_casts]
    cast_out = refs[n_casts + 1:2 * n_casts + 1]
    m_ref, l_ref, acc_ref = refs[2 * n_casts + 1:]
    step = (pl.program_id(0) * pl.num_programs(1) + pl.program_id(1)) * pl.num_programs(2) + pl.program_id(2)
    for src, dst, n_blocks in zip(cast_in, cast_out, cast_blocks):
        @pl.when(step < n_blocks)
        def _(src=src, dst=dst):
            dst[...] = src[...].astype(dst.dtype)
    m_ref[...] = jnp.full(m_ref.shape, -jnp.inf, F32)
    l_ref[...] = jnp.zeros(l_ref.shape, F32)
    acc_ref[...] = jnp.zeros(acc_ref.shape, F32)

    def scores(k):
        out = []
        for m in range(2):
            cols = slice(m * HEAD_DIM, (m + 1) * HEAD_DIM)
            out.append(lax.dot_general(q_ref[:, cols], k[:, cols], (((1,), (1,)), ((), ())),
                                       preferred_element_type=F32))
        return out

    def softmax_pv(s_maps, v):
        for m, s in enumerate(s_maps):
            n = s.shape[1]
            m_prev = m_ref[m]
            m_next = jnp.maximum(m_prev, jnp.max(s, axis=1, keepdims=True))
            alpha = jnp.exp2(m_prev - m_next)
            p = jnp.exp2(s - jnp.concatenate([m_next] * (n // LANES), axis=1))
            l_ref[m] = alpha * l_ref[m] + jnp.sum(p, axis=1, keepdims=True)
            m_ref[m] = m_next
            pv = jnp.dot(p.astype(BF16), v, preferred_element_type=F32)
            acc_ref[m] = acc_ref[m] * jnp.concatenate([alpha] * (V_HEAD_DIM // LANES), axis=1) + pv

    chunks = [(kc_ref, vc_ref, pl.ds(0, n_ctx_keys))]
    chunks += [(kl_ref, vl_ref, pl.ds(c * tk, tk)) for c in range(n_lat_keys // tk)]
    s_next = scores(chunks[0][0][chunks[0][2], :])
    for c, (_, v_ref, rows) in enumerate(chunks):
        s_cur = s_next
        if c + 1 < len(chunks):
            k_next, _, rows_next = chunks[c + 1]
            s_next = scores(k_next[rows_next, :])
        softmax_pv(s_cur, v_ref[rows, :])

    lp = lam_ref[...]
    lam = (jnp.exp(jnp.sum(lp[0:1] * lp[1:2], axis=1, keepdims=True))
           - jnp.exp(jnp.sum(lp[2:3] * lp[3:4], axis=1, keepdims=True)) + lam_init)
    rep = V_HEAD_DIM // LANES
    o1 = acc_ref[0] / jnp.concatenate([l_ref[0]] * rep, axis=1)
    o2 = acc_ref[1] / jnp.concatenate([l_ref[1]] * rep, axis=1)
    o = o1 - lam * o2
    ms = jnp.mean(o * o, axis=-1, keepdims=True)
    y = o * lax.rsqrt(ms + SUBLN_EPS) * g_ref[...]
    o_ref[...] = (y * (1.0 - lam_init)).astype(o_ref.dtype)


def _attn_call(qkv, lam_p, subln_g, lam_init, latent, casts=()):
    hq = D_MODEL // V_HEAD_DIM
    ctx_row0 = N_LAT // CTX_LEN
    if latent:
        tq, tk, n_lat_keys, n_out = 512, 1024, SEQ, N_LAT
        n_q = SEQ // tq
        q_map = lambda b, h, qi: (b * (SEQ // tq) + qi, h)
    else:
        tq, tk, n_q, n_lat_keys, n_out = CTX_LEN, 512, 1, 0, N_CTX
        q_map = lambda b, h, qi: (ctx_row0 + b, h)
    in_specs = [
        pl.BlockSpec((4, HEAD_DIM), lambda b, h, qi: (0, 0)),
        pl.BlockSpec((1, V_HEAD_DIM), lambda b, h, qi: (0, 0)),
        pl.BlockSpec((tq, V_HEAD_DIM), q_map),
        pl.BlockSpec((CTX_LEN, V_HEAD_DIM), lambda b, h, qi: (ctx_row0 + b, hq + h)),
        pl.BlockSpec((CTX_LEN, V_HEAD_DIM), lambda b, h, qi: (ctx_row0 + b, 2 * hq + h)),
    ]
    args = [lam_p, subln_g.reshape(1, V_HEAD_DIM), qkv, qkv, qkv]
    if latent:
        in_specs += [
            pl.BlockSpec((SEQ, V_HEAD_DIM), lambda b, h, qi: (b, hq + h)),
            pl.BlockSpec((SEQ, V_HEAD_DIM), lambda b, h, qi: (b, 2 * hq + h)),
        ]
        args += [qkv, qkv]
        out_map = lambda b, h, qi: (b * (SEQ // tq) + qi, h)
    else:
        out_map = lambda b, h, qi: (b, h)
    n_steps = BATCH * N_HEADS * n_q
    out_specs = [pl.BlockSpec((tq, V_HEAD_DIM), out_map)]
    out_shape = [jax.ShapeDtypeStruct((n_out, D_MODEL), BF16)]
    cast_shapes, cast_blocks = [], []
    for w, layer in casts:
        rows, cols = math.prod(w.shape[1:-1]), w.shape[-1]
        n_blocks = rows // BF16_SUBLANES
        assert rows % BF16_SUBLANES == 0 and n_blocks <= n_steps

        def block(b, h, qi, n_blocks=n_blocks):
            return jnp.minimum((b * N_HEADS + h) * n_q + qi, n_blocks - 1)

        in_specs.append(pl.BlockSpec((None, BF16_SUBLANES, cols),
                                     lambda b, h, qi, layer=layer, block=block: (layer, block(b, h, qi), 0)))
        args.append(w.reshape(w.shape[0], rows, cols))
        out_specs.append(pl.BlockSpec((BF16_SUBLANES, cols), lambda b, h, qi, block=block: (block(b, h, qi), 0)))
        out_shape.append(jax.ShapeDtypeStruct((rows, cols), BF16))
        cast_shapes.append((1,) + w.shape[1:])
        cast_blocks.append(n_blocks)
    kern = functools.partial(_attn_kernel, tq=tq, tk=tk, n_ctx_keys=CTX_LEN, n_lat_keys=n_lat_keys,
                             cast_blocks=tuple(cast_blocks), lam_init=lam_init)
    outs = pl.pallas_call(
        kern,
        grid=(BATCH, N_HEADS, n_q),
        in_specs=in_specs,
        out_specs=out_specs,
        out_shape=out_shape,
        scratch_shapes=[
            pltpu.VMEM((2, tq, LANES), F32),
            pltpu.VMEM((2, tq, LANES), F32),
            pltpu.VMEM((2, tq, V_HEAD_DIM), F32),
        ],
        compiler_params=_params(("arbitrary", "arbitrary", "arbitrary"), 48 << 20),
        name="diff_attn_lat" if latent else "diff_attn_ctx",
    )(*args)
    return outs[0], [o.reshape(s) for o, s in zip(outs[1:], cast_shapes)]


def _mm_res_kernel(a_ref, w_ref, x_ref, gate_ref, cs_ref, o_ref, *, chunk):
    tm = o_ref.shape[0]
    for c in range(tm // chunk):
        rows = pl.ds(c * chunk, chunk)
        acc = jnp.dot(a_ref[rows, :], w_ref[...], preferred_element_type=F32)
        o_ref[rows, :] = x_ref[rows, :] + gate_ref[...] * (acc * cs_ref[...])


def _mm_res_call(a, w, w_idx, xs, modr, col_scale, layer, gate_chunk, row0, *, tm, tn, grouped=False):
    k = w.shape[-2]
    assert a.shape[0] % tm == 0 and row0 % tm == 0
    n_row_tiles = a.shape[0] // tm
    row_tile0 = row0 // tm
    gate = _mod_row(layer, gate_chunk, tm)
    if grouped:
        assert tn == w.shape[-1]
        a_spec = pl.BlockSpec((tm, k), lambda i, j: (i, j))
        w_spec = pl.BlockSpec((None, None, k, tn), lambda i, j: (w_idx, j, 0, 0))
    else:
        a_spec = pl.BlockSpec((tm, k), lambda i, j: (i, 0))
        w_spec = pl.BlockSpec((None, k, tn), lambda i, j: (w_idx, 0, j))
    x_spec = pl.BlockSpec((tm, tn), lambda i, j: (row_tile0 + i, j))
    vmem = 2 * (tm * k * 2 + k * tn * 2 + 2 * tm * tn * 4) + (12 << 20)
    return pl.pallas_call(
        functools.partial(_mm_res_kernel, chunk=min(tm, 512)),
        grid=(n_row_tiles, D_MODEL // tn),
        in_specs=[
            a_spec,
            w_spec,
            x_spec,
            pl.BlockSpec((None, 1, tn), lambda i, j: (gate(row_tile0 + i), 0, j)),
            pl.BlockSpec((1, tn), lambda i, j: (0, j)),
        ],
        out_specs=x_spec,
        out_shape=jax.ShapeDtypeStruct(xs.shape, F32),
        input_output_aliases={2: 0},
        compiler_params=_params(("arbitrary", "arbitrary"), vmem),
        name="mm_residual",
    )(a, w, xs, modr, col_scale)


def _pool_kernel(x_ref, xp_ref, xn_ref, g_ref, sh_ref, sc_ref, o_ref, hs_ref, *, r, halo, n_lat_tiles):
    i = pl.program_id(0)
    seq_len = jnp.where(i < n_lat_tiles, SEQ, CTX_LEN)
    pos0 = (i * r) & (seq_len - 1)
    is_first = pos0 == 0
    is_last = pos0 + r == seq_len
    g, sh, sc = g_ref[...], sh_ref[...], sc_ref[...]
    hs_ref[0:halo, :] = jnp.where(is_first, 0.0, _norm_mod(xp_ref[...], g, sh, sc))
    hs_ref[halo:halo + r, :] = _norm_mod(x_ref[...], g, sh, sc)
    hs_ref[halo + r:, :] = jnp.where(is_last, 0.0, _norm_mod(xn_ref[...], g, sh, sc))

    rc, cc = 128, 256
    for grp, w in enumerate(POOL_WINDOWS):
        before, after = w // 2, w - 1 - w // 2
        for ri in range(r // rc):
            pos = pos0 + ri * rc + lax.broadcasted_iota(jnp.int32, (rc, 1), 0)
            lo = jnp.maximum(pos - before, 0)
            hi = jnp.minimum(pos + after, seq_len - 1)
            inv_cnt = 1.0 / (hi - lo + 1).astype(F32)
            for ci in range(POOL_GROUP // cc):
                cols = slice(grp * POOL_GROUP + ci * cc, grp * POOL_GROUP + (ci + 1) * cc)
                acc = hs_ref[pl.ds(halo + ri * rc - before, rc), cols]
                for s in range(-before + 1, after + 1):
                    acc = acc + hs_ref[pl.ds(halo + ri * rc + s, rc), cols]
                y = acc * inv_cnt - hs_ref[pl.ds(halo + ri * rc, rc), cols]
                o_ref[pl.ds(ri * rc, rc), cols] = y.astype(o_ref.dtype)


def _pool_call(xs, g, modr, layer, n_rows):
    r, halo = 256, 8
    n_halo_blocks = n_rows // halo
    sh = _mod_row(layer, 0, r)
    sc = _mod_row(layer, 1, r)
    kern = functools.partial(_pool_kernel, r=r, halo=halo, n_lat_tiles=N_LAT // r)
    return pl.pallas_call(
        kern,
        grid=(n_rows // r,),
        in_specs=[
            pl.BlockSpec((r, D_MODEL), lambda i: (i, 0)),
            pl.BlockSpec((halo, D_MODEL), lambda i: (jnp.maximum(i * (r // halo) - 1, 0), 0)),
            pl.BlockSpec((halo, D_MODEL), lambda i: (jnp.minimum((i + 1) * (r // halo), n_halo_blocks - 1), 0)),
            pl.BlockSpec((1, D_MODEL), lambda i: (0, 0)),
            pl.BlockSpec((None, 1, D_MODEL), lambda i: (sh(i), 0, 0)),
            pl.BlockSpec((None, 1, D_MODEL), lambda i: (sc(i), 0, 0)),
        ],
        out_specs=pl.BlockSpec((r, D_MODEL), lambda i: (i, 0)),
        out_shape=jax.ShapeDtypeStruct((n_rows, D_MODEL), BF16),
        scratch_shapes=[pltpu.VMEM((r + 2 * halo, D_MODEL), F32)],
        compiler_params=_params(("arbitrary",), 32 << 20),
        name="norm_mod_pool",
    )(xs, xs, xs, g.reshape(1, D_MODEL), modr, modr)


def _ffn_in_kernel(a_ref, ap_ref, an_ref, wg_ref, wv_ref, cwg_ref, cwv_ref, cbg_ref, cbv_ref, o_ref,
                   aext_ref, u_ref, *, halo, n_lat_tiles, dot_chunks, rc):
    i = pl.program_id(0)
    j = pl.program_id(1)
    tm, half = o_ref.shape

    @pl.when(j == 0)
    def _():
        aext_ref[0:halo, :] = ap_ref[...]
        aext_ref[halo:halo + tm, :] = a_ref[...]
        aext_ref[halo + tm:, :] = an_ref[...]

    row = 0
    for size in dot_chunks:
        rows = pl.ds(row, size)
        a = aext_ref[rows, :]
        u_ref[rows, 0:half] = jnp.dot(a, wg_ref[...], preferred_element_type=F32)
        u_ref[rows, half:] = jnp.dot(a, wv_ref[...], preferred_element_type=F32)
        row += size

    seq_len = jnp.where(i < n_lat_tiles, SEQ, CTX_LEN)
    cw = jnp.concatenate([cwg_ref[...], cwv_ref[...]], axis=1)
    cb = jnp.concatenate([cbg_ref[...], cbv_ref[...]], axis=1)
    sub = lax.broadcasted_iota(jnp.int32, (8, 1), 0)
    for ri in range(tm // rc):
        r0 = ri * rc
        up = u_ref[pl.ds(halo + r0 - 1, rc), :]
        un = u_ref[pl.ds(halo + r0 + 1, rc), :]
        uc = u_ref[pl.ds(halo + r0, rc), :]
        if r0 % CTX_LEN == 0:
            pos = (i * tm + r0 + sub) & (seq_len - 1)
            up = jnp.concatenate([jnp.where(pos != 0, up[0:8], 0.0), up[8:]], axis=0)
        if (r0 + rc) % CTX_LEN == 0:
            pos = (i * tm + r0 + rc - 8 + sub) & (seq_len - 1)
            un = jnp.concatenate([un[:rc - 8], jnp.where(pos != seq_len - 1, un[rc - 8:], 0.0)], axis=0)
        conv = up * cw[0:1] + uc * cw[1:2] + un * cw[2:3] + cb
        gate = conv[:, :half]
        val = conv[:, half:]
        o_ref[pl.ds(r0, rc), :] = (gate * (1.0 / (1.0 + jnp.exp(-gate))) * val).astype(o_ref.dtype)


def _ffn_in_call(h, w_in, w_idx, conv_w, conv_b, layer, n_rows):
    half, halo = 512, BF16_SUBLANES
    n_halo_blocks = n_rows // halo
    ext = TM + 2 * halo
    nj = D_FF // half
    kern = functools.partial(_ffn_in_kernel, halo=halo, n_lat_tiles=N_LAT // TM,
                             dot_chunks=(144, 144, 128, 128, 128, 128, 128, 128), rc=32)
    vmem = (2 * (TM * D_MODEL * 2 + 2 * D_MODEL * half * 2 + TM * half * 2 + 2 * halo * D_MODEL * 2)
            + ext * D_MODEL * 2 + ext * 2 * half * 4 + (12 << 20))
    return pl.pallas_call(
        kern,
        grid=(n_rows // TM, nj),
        in_specs=[
            pl.BlockSpec((TM, D_MODEL), lambda i, j: (i, 0)),
            pl.BlockSpec((halo, D_MODEL), lambda i, j: (jnp.maximum(i * (TM // halo) - 1, 0), 0)),
            pl.BlockSpec((halo, D_MODEL),
                         lambda i, j: (jnp.minimum((i + 1) * (TM // halo), n_halo_blocks - 1), 0)),
            pl.BlockSpec((None, D_MODEL, half), lambda i, j: (w_idx, 0, j)),
            pl.BlockSpec((None, D_MODEL, half), lambda i, j: (w_idx, 0, nj + j)),
            pl.BlockSpec((None, 3, half), lambda i, j: (layer, 0, j)),
            pl.BlockSpec((None, 3, half), lambda i, j: (layer, 0, nj + j)),
            pl.BlockSpec((None, 1, half), lambda i, j: (layer, 0, j)),
            pl.BlockSpec((None, 1, half), lambda i, j: (layer, 0, nj + j)),
        ],
        out_specs=pl.BlockSpec((TM, half), lambda i, j: (i, j)),
        out_shape=jax.ShapeDtypeStruct((n_rows, D_FF), BF16),
        scratch_shapes=[
            pltpu.VMEM((ext, D_MODEL), BF16),
            pltpu.VMEM((ext, 2 * half), F32),
        ],
        compiler_params=_params(("arbitrary", "arbitrary"), vmem),
        name="ffn_in_conv_glu",
    )(h, h, h, w_in, w_in, conv_w, conv_w, conv_b, conv_b)


def _rope_tables():
    rows = SEQ // GRID_W
    row = jnp.repeat(jnp.arange(rows), GRID_W).astype(F32)
    col = jnp.tile(jnp.arange(GRID_W), rows).astype(F32)
    inv = 1.0 / (ROPE_BASE ** (jnp.arange(0, ROPE_HALF, 2, dtype=F32) / ROPE_HALF))

    def tab(pos):
        a = pos[:, None] * inv[None, :]
        a = jnp.concatenate([a, a], axis=-1)
        return jnp.cos(a), jnp.sin(a)

    cr, sr = tab(row)
    cc, sc = tab(col)
    cos = jnp.concatenate([cr, cc], axis=-1)
    sin = jnp.concatenate([sr, sc], axis=-1)
    lane = jnp.arange(HEAD_DIM)
    sin = jnp.where((lane % ROPE_HALF) < ROPE_HALF // 2, -sin, sin)
    cos_all = jnp.concatenate([jnp.tile(cos, (BATCH, 1)), jnp.ones((N_CTX, HEAD_DIM), F32)], axis=0)
    sin_all = jnp.concatenate([jnp.tile(sin, (BATCH, 1)), jnp.zeros((N_CTX, HEAD_DIM), F32)], axis=0)
    return cos_all, sin_all


def _ctx_needed_after(i):
    return any((j % N_MIXERS) == 0 for j in range(i + 1, DEPTH))


def kernel(x, c, ctx, c_ctx, ada_w, ada_b, norm_g, attn_qkv, attn_o, attn_lambda, attn_subln_g,
           pool_w, pool_scale, ffn_w_in, ffn_conv_w, ffn_conv_b, ffn_w_out, final_g):
    assert x.shape == (BATCH, SEQ, D_MODEL) and ctx.shape == (BATCH, CTX_LEN, D_MODEL)
    xs = jnp.concatenate([x.reshape(N_LAT, D_MODEL), ctx.reshape(N_CTX, D_MODEL)], axis=0)
    cond = jnp.concatenate(
        [c, c_ctx[None, :], jnp.zeros((COND_ROWS - BATCH - 1, D_MODEL), F32)], axis=0)
    mod = _ada(cond, ada_w, ada_b)
    modr = mod.reshape(DEPTH, COND_ROWS, 6, D_MODEL).transpose(0, 2, 1, 3).reshape(
        DEPTH * 6 * COND_ROWS, 1, D_MODEL)
    cos_tab, sin_tab = _rope_tables()
    ones_row = jnp.ones((1, D_MODEL), F32)

    conv_b = ffn_conv_b.reshape(DEPTH, 1, 2 * D_FF)
    w16 = {("qkv", 0): attn_qkv[0:1].astype(BF16)}

    for i in range(DEPTH):
        need_ctx = _ctx_needed_after(i)
        n_rows = N_ALL if need_ctx else N_LAT
        if i % N_MIXERS == 0:
            a = i // N_MIXERS
            lam_init = 0.8 - 0.6 * math.exp(-0.3 * i)
            h = _norm_mod_call(xs, norm_g[i, 0], modr, i, 0, N_ALL)
            qkv = _qkv_call(h, w16["qkv", a], 0, cos_tab, sin_tab, N_ALL)
            casts = [("o", a, attn_o), ("in", i, ffn_w_in), ("out", i, ffn_w_out)]
            for nxt in range(i + 1, min(i + N_MIXERS, DEPTH)):
                casts += [("pool", nxt // N_MIXERS, pool_w), ("in", nxt, ffn_w_in), ("out", nxt, ffn_w_out)]
            if i + N_MIXERS < DEPTH:
                casts.append(("qkv", a + 1, attn_qkv))
            o_lat, cast_out = _attn_call(qkv, attn_lambda[a], attn_subln_g[a], lam_init, latent=True,
                                         casts=[(w, idx) for _, idx, w in casts])
            for (name, idx, _), w_bf16 in zip(casts, cast_out):
                w16[name, idx] = w_bf16
            xs = _mm_res_call(o_lat, w16["o", a], 0, xs, modr, ones_row, i, 2, 0, tm=TM, tn=512)
            if need_ctx:
                o_ctx, _ = _attn_call(qkv, attn_lambda[a], attn_subln_g[a], lam_init, latent=False)
                xs = _mm_res_call(o_ctx, w16["o", a], 0, xs, modr, ones_row, i, 2, N_LAT, tm=TM, tn=512)
        else:
            p = i // N_MIXERS
            y = _pool_call(xs, norm_g[i, 0], modr, i, n_rows)
            xs = _mm_res_call(y, w16["pool", p], 0, xs, modr, pool_scale[p].reshape(1, D_MODEL), i, 2, 0,
                              tm=TM, tn=POOL_GROUP, grouped=True)
        h = _norm_mod_call(xs, norm_g[i, 1], modr, i, 3, n_rows)
        act = _ffn_in_call(h, w16["in", i], 0, ffn_conv_w, conv_b, i, n_rows)
        xs = _mm_res_call(act, w16["out", i], 0, xs, modr, ones_row, i, 5, 0, tm=TM, tn=512)
    return _final_norm(xs, final_g).reshape(BATCH, SEQ, D_MODEL)
```

```python
import functools
import math

import jax
import jax.numpy as jnp
from jax import lax
from jax.experimental import pallas as pl
from jax.experimental.pallas import tpu as pltpu

D_MODEL = 4096
BATCH = 4
SEQ = 4096
DEPTH = 4
GRID_W = 64
CTX_LEN = 256
N_MIXERS = 2
HEAD_DIM = 128
V_HEAD_DIM = 2 * HEAD_DIM
N_HEADS = D_MODEL // V_HEAD_DIM
ROPE_HALF = HEAD_DIM // 2
ROPE_BASE = 10000.0
POOL_WINDOWS = (2, 4, 8, 16)
N_POOL_GROUPS = len(POOL_WINDOWS)
POOL_GROUP = D_MODEL // N_POOL_GROUPS
D_FF = 7168
EPS = 1e-6
SUBLN_EPS = 1e-5

N_LAT = BATCH * SEQ
N_CTX = BATCH * CTX_LEN
N_ALL = N_LAT + N_CTX
COND_ROWS = 8

V7X_VMEM_BYTES = 64 * 1024 * 1024
VMEM_CAP_BYTES = V7X_VMEM_BYTES - 6 * 1024 * 1024
LANES = 128
BF16_SUBLANES = 16
LOG2_E = math.log2(math.e)

TM = 1024
BF16 = jnp.bfloat16
F32 = jnp.float32


def _params(semantics, vmem_bytes, flags=None):
    return pltpu.CompilerParams(
        dimension_semantics=semantics,
        vmem_limit_bytes=int(min(VMEM_CAP_BYTES, vmem_bytes)),
        flags=flags,
    )


def _mod_row(layer, chunk, tile_rows):
    n_lat_tiles = N_LAT // tile_rows
    tiles_per_seq = SEQ // tile_rows
    base = (layer * 6 + chunk) * COND_ROWS

    def idx(i):
        return base + jnp.where(i < n_lat_tiles, i // tiles_per_seq, BATCH)

    return idx


def _ada_kernel(cond_ref, w_ref, b_ref, o_ref):
    k = pl.program_id(2)
    c = cond_ref[...]
    c = c * (1.0 / (1.0 + jnp.exp(-c)))
    acc = jnp.dot(c.astype(BF16), w_ref[...].astype(BF16), preferred_element_type=F32)

    @pl.when(k == 0)
    def _():
        o_ref[...] = acc + b_ref[...]

    @pl.when(k != 0)
    def _():
        o_ref[...] += acc


def _ada(cond, ada_w, ada_b):
    tn, tk = 2048, 1024
    n = 6 * D_MODEL
    return pl.pallas_call(
        _ada_kernel,
        grid=(DEPTH, n // tn, D_MODEL // tk),
        in_specs=[
            pl.BlockSpec((COND_ROWS, tk), lambda i, j, k: (0, k)),
            pl.BlockSpec((None, tk, tn), lambda i, j, k: (i, k, j)),
            pl.BlockSpec((None, 1, tn), lambda i, j, k: (i, 0, j)),
        ],
        out_specs=pl.BlockSpec((None, COND_ROWS, tn), lambda i, j, k: (i, 0, j)),
        out_shape=jax.ShapeDtypeStruct((DEPTH, COND_ROWS, n), F32),
        compiler_params=_params(("arbitrary", "arbitrary", "arbitrary"), 3 * tk * tn * 4 + (8 << 20)),
        name="ada_mod",
    )(cond, ada_w, ada_b.reshape(DEPTH, 1, n))


def _norm_mod(x, g, sh, sc):
    ms = jnp.mean(x * x, axis=-1, keepdims=True)
    y = x * lax.rsqrt(ms + EPS) * g
    return y * (1.0 + sc) + sh


def _norm_mod_kernel(x_ref, g_ref, sh_ref, sc_ref, o_ref):
    o_ref[...] = _norm_mod(x_ref[...], g_ref[...], sh_ref[...], sc_ref[...]).astype(o_ref.dtype)


def _norm_mod_call(xs, g, modr, layer, chunk0, n_rows):
    r = 512
    sh = _mod_row(layer, chunk0, r)
    sc = _mod_row(layer, chunk0 + 1, r)
    return pl.pallas_call(
        _norm_mod_kernel,
        grid=(n_rows // r,),
        in_specs=[
            pl.BlockSpec((r, D_MODEL), lambda i: (i, 0)),
            pl.BlockSpec((1, D_MODEL), lambda i: (0, 0)),
            pl.BlockSpec((None, 1, D_MODEL), lambda i: (sh(i), 0, 0)),
            pl.BlockSpec((None, 1, D_MODEL), lambda i: (sc(i), 0, 0)),
        ],
        out_specs=pl.BlockSpec((r, D_MODEL), lambda i: (i, 0)),
        out_shape=jax.ShapeDtypeStruct((n_rows, D_MODEL), BF16),
        compiler_params=_params(("arbitrary",), 6 * r * D_MODEL * 4),
        name="norm_mod",
    )(xs, g.reshape(1, D_MODEL), modr, modr)


def _final_norm_kernel(x_ref, g_ref, o_ref):
    x = x_ref[...]
    ms = jnp.mean(x * x, axis=-1, keepdims=True)
    o_ref[...] = x * lax.rsqrt(ms + EPS) * g_ref[...]


def _final_norm(xs, g):
    r = 512
    return pl.pallas_call(
        _final_norm_kernel,
        grid=(N_LAT // r,),
        in_specs=[
            pl.BlockSpec((r, D_MODEL), lambda i: (i, 0)),
            pl.BlockSpec((1, D_MODEL), lambda i: (0, 0)),
        ],
        out_specs=pl.BlockSpec((r, D_MODEL), lambda i: (i, 0)),
        out_shape=jax.ShapeDtypeStruct((N_LAT, D_MODEL), F32),
        compiler_params=_params(("arbitrary",), 6 * r * D_MODEL * 4),
        name="final_norm",
    )(xs, g.reshape(1, D_MODEL))


def _qkv_kernel(a_ref, w_ref, cos_ref, sin_ref, o_ref, u0_ref, u1_ref, *, n_col_tiles, dot_chunks, rc,
                n_rope_blocks, n_q_blocks, q_scale):
    t = pl.program_id(0)
    tm, tn = o_ref.shape
    je = jnp.maximum(t - 1, 0) % n_col_tiles
    use_rope = je < n_rope_blocks
    qs = jnp.where(je < n_q_blocks, q_scale, 1.0).astype(F32)
    lane = lax.broadcasted_iota(jnp.int32, (rc, HEAD_DIM), 1)
    first_half = (lane % ROPE_HALF) < (ROPE_HALF // 2)

    @pl.when(t == 0)
    def _():
        u1_ref[...] = jnp.zeros(u1_ref.shape, F32)

    def work(uw_ref, ur_ref):
        per_chunk = (tm // rc) // len(dot_chunks)
        row = 0
        for c, size in enumerate(dot_chunks):
            for ri in range(c * per_chunk, (c + 1) * per_chunk):
                rows = pl.ds(ri * rc, rc)
                cos = jnp.where(use_rope, cos_ref[rows, :], 1.0)
                sin = jnp.where(use_rope, sin_ref[rows, :], 0.0)
                for ci in range(tn // HEAD_DIM):
                    cols = slice(ci * HEAD_DIM, (ci + 1) * HEAD_DIM)
                    x = ur_ref[rows, cols]
                    rot = jnp.where(first_half, pltpu.roll(x, HEAD_DIM - ROPE_HALF // 2, 1),
                                    pltpu.roll(x, ROPE_HALF // 2, 1))
                    o_ref[rows, cols] = ((x * cos + rot * sin) * qs).astype(o_ref.dtype)
            rows = pl.ds(row, size)
            uw_ref[rows, :] = jnp.dot(a_ref[rows, :], w_ref[...], preferred_element_type=F32)
            row += size

    @pl.when(t % 2 == 0)
    def _():
        work(u0_ref, u1_ref)

    @pl.when(t % 2 == 1)
    def _():
        work(u1_ref, u0_ref)


def _qkv_call(h, w, layer, cos_tab, sin_tab, n_rows):
    tn = 1024
    n = 3 * D_MODEL
    nj = n // tn
    n_steps = (n_rows // TM) * nj
    kern = functools.partial(
        _qkv_kernel, n_col_tiles=nj, dot_chunks=(256, 256, 256, 256), rc=64,
        n_rope_blocks=2 * D_MODEL // tn, n_q_blocks=D_MODEL // tn, q_scale=HEAD_DIM ** -0.5 * LOG2_E)
    vmem = (2 * (TM * D_MODEL * 2 + D_MODEL * tn * 2 + TM * tn * 2 + 2 * TM * HEAD_DIM * 4)
            + 2 * TM * tn * 4 + (12 << 20))

    def mm(t):
        tt = jnp.minimum(t, n_steps - 1)
        return tt // nj, tt % nj

    def ep(t):
        te = jnp.maximum(t - 1, 0)
        return te // nj, te % nj

    return pl.pallas_call(
        kern,
        grid=(n_steps + 1,),
        in_specs=[
            pl.BlockSpec((TM, D_MODEL), lambda t: (mm(t)[0], 0)),
            pl.BlockSpec((None, D_MODEL, tn), lambda t: (layer, 0, mm(t)[1])),
            pl.BlockSpec((TM, HEAD_DIM), lambda t: (ep(t)[0], 0)),
            pl.BlockSpec((TM, HEAD_DIM), lambda t: (ep(t)[0], 0)),
        ],
        out_specs=pl.BlockSpec((TM, tn), lambda t: ep(t)),
        out_shape=jax.ShapeDtypeStruct((n_rows, n), BF16),
        scratch_shapes=[pltpu.VMEM((TM, tn), F32), pltpu.VMEM((TM, tn), F32)],
        compiler_params=_params(("arbitrary",), vmem),
        name="qkv_rope",
    )(h, w, cos_tab, sin_tab)


def _attn_kernel(lam_ref, g_ref, q_ref, *refs, tq, tk, n_ctx_keys, n_lat_keys, cast_blocks, lam_init):
    refs = list(refs)
    n_casts = len(cast_blocks)
    kc_ref, vc_ref = refs[:2]
    del refs[:2]
    if n_lat_keys:
        kl_ref, vl_ref = refs[:2]
        del refs[:2]
    cast_in = refs[:n_casts]
    o_ref = refs[n_casts]
    cast_out = refs[n_casts + 1:2 * n_casts + 1]
    m_ref, l_ref, acc_ref = refs[2 * n_casts + 1:]
    step = (pl.program_id(0) * pl.num_programs(1) + pl.program_id(1)) * pl.num_programs(2) + pl.program_id(2)
    for src, dst, n_blocks in zip(cast_in, cast_out, cast_blocks):
        @pl.when(step < n_blocks)
        def _(src=src, dst=dst):
            dst[...] = src[...].astype(dst.dtype)
    m_ref[...] = jnp.full(m_ref.shape, -jnp.inf, F32)
    l_ref[...] = jnp.zeros(l_ref.shape, F32)
    acc_ref[...] = jnp.zeros(acc_ref.shape, F32)

    def scores(k):
        out = []
        for m in range(2):
            cols = slice(m * HEAD_DIM, (m + 1) * HEAD_DIM)
            out.append(lax.dot_general(q_ref[:, cols], k[:, cols], (((1,), (1,)), ((), ())),
                                       preferred_element_type=F32))
        return out

    def softmax_pv(s_maps, v):
        for m, s in enumerate(s_maps):
            n = s.shape[1]
            m_prev = m_ref[m]
            m_next = jnp.maximum(m_prev, jnp.max(s, axis=1, keepdims=True))
            alpha = jnp.exp2(m_prev - m_next)
            p = jnp.exp2(s - jnp.concatenate([m_next] * (n // LANES), axis=1))
            l_ref[m] = alpha * l_ref[m] + jnp.sum(p, axis=1, keepdims=True)
            m_ref[m] = m_next
            pv = jnp.dot(p.astype(BF16), v, preferred_element_type=F32)
            acc_ref[m] = acc_ref[m] * jnp.concatenate([alpha] * (V_HEAD_DIM // LANES), axis=1) + pv

    chunks = [(kc_ref, vc_ref, pl.ds(0, n_ctx_keys))]
    chunks += [(kl_ref, vl_ref, pl.ds(c * tk, tk)) for c in range(n_lat_keys // tk)]
    s_next = scores(chunks[0][0][chunks[0][2], :])
    for c, (_, v_ref, rows) in enumerate(chunks):
        s_cur = s_next
        if c + 1 < len(chunks):
            k_next, _, rows_next = chunks[c + 1]
            s_next = scores(k_next[rows_next, :])
        softmax_pv(s_cur, v_ref[rows, :])

    lp = lam_ref[...]
    lam = (jnp.exp(jnp.sum(lp[0:1] * lp[1:2], axis=1, keepdims=True))
           - jnp.exp(jnp.sum(lp[2:3] * lp[3:4], axis=1, keepdims=True)) + lam_init)
    rep = V_HEAD_DIM // LANES
    o1 = acc_ref[0] / jnp.concatenate([l_ref[0]] * rep, axis=1)
    o2 = acc_ref[1] / jnp.concatenate([l_ref[1]] * rep, axis=1)
    o = o1 - lam * o2
    ms = jnp.mean(o * o, axis=-1, keepdims=True)
    y = o * lax.rsqrt(ms + SUBLN_EPS) * g_ref[...]
    o_ref[...] = (y * (1.0 - lam_init)).astype(o_ref.dtype)


def _attn_call(qkv, lam_p, subln_g, lam_init, latent, casts=()):
    hq = D_MODEL // V_HEAD_DIM
    ctx_row0 = N_LAT // CTX_LEN
    if latent:
        tq, tk, n_lat_keys, n_out = 512, 1024, SEQ, N_LAT
        n_q = SEQ // tq
        q_map = lambda b, h, qi: (b * (SEQ // tq) + qi, h)
    else:
        tq, tk, n_q, n_lat_keys, n_out = CTX_LEN, 512, 1, 0, N_CTX
        q_map = lambda b, h, qi: (ctx_row0 + b, h)
    in_specs = [
        pl.BlockSpec((4, HEAD_DIM), lambda b, h, qi: (0, 0)),
        pl.BlockSpec((1, V_HEAD_DIM), lambda b, h, qi: (0, 0)),
        pl.BlockSpec((tq, V_HEAD_DIM), q_map),
        pl.BlockSpec((CTX_LEN, V_HEAD_DIM), lambda b, h, qi: (ctx_row0 + b, hq + h)),
        pl.BlockSpec((CTX_LEN, V_HEAD_DIM), lambda b, h, qi: (ctx_row0 + b, 2 * hq + h)),
    ]
    args = [lam_p, subln_g.reshape(1, V_HEAD_DIM), qkv, qkv, qkv]
    if latent:
        in_specs += [
            pl.BlockSpec((SEQ, V_HEAD_DIM), lambda b, h, qi: (b, hq + h)),
            pl.BlockSpec((SEQ, V_HEAD_DIM), lambda b, h, qi: (b, 2 * hq + h)),
        ]
        args += [qkv, qkv]
        out_map = lambda b, h, qi: (b * (SEQ // tq) + qi, h)
    else:
        out_map = lambda b, h, qi: (b, h)
    n_steps = BATCH * N_HEADS * n_q
    out_specs = [pl.BlockSpec((tq, V_HEAD_DIM), out_map)]
    out_shape = [jax.ShapeDtypeStruct((n_out, D_MODEL), BF16)]
    cast_shapes, cast_blocks = [], []
    for w, layer in casts:
        rows, cols = math.prod(w.shape[1:-1]), w.shape[-1]
        n_blocks = rows // BF16_SUBLANES
        assert rows % BF16_SUBLANES == 0 and n_blocks <= n_steps

        def block(b, h, qi, n_blocks=n_blocks):
            return jnp.minimum((b * N_HEADS + h) * n_q + qi, n_blocks - 1)

        in_specs.append(pl.BlockSpec((None, BF16_SUBLANES, cols),
                                     lambda b, h, qi, layer=layer, block=block: (layer, block(b, h, qi), 0)))
        args.append(w.reshape(w.shape[0], rows, cols))
        out_specs.append(pl.BlockSpec((BF16_SUBLANES, cols), lambda b, h, qi, block=block: (block(b, h, qi), 0)))
        out_shape.append(jax.ShapeDtypeStruct((rows, cols), BF16))
        cast_shapes.append((1,) + w.shape[1:])
        cast_blocks.append(n_blocks)
    kern = functools.partial(_attn_kernel, tq=tq, tk=tk, n_ctx_keys=CTX_LEN, n_lat_keys=n_lat_keys,
                             cast_blocks=tuple(cast_blocks), lam_init=lam_init)
    outs = pl.pallas_call(
        kern,
        grid=(BATCH, N_HEADS, n_q),
        in_specs=in_specs,
        out_specs=out_specs,
        out_shape=out_shape,
        scratch_shapes=[
            pltpu.VMEM((2, tq, LANES), F32),
            pltpu.VMEM((2, tq, LANES), F32),
            pltpu.VMEM((2, tq, V_HEAD_DIM), F32),
        ],
        compiler_params=_params(("arbitrary", "arbitrary", "arbitrary"), 48 << 20),
        name="diff_attn_lat" if latent else "diff_attn_ctx",
    )(*args)
    return outs[0], [o.reshape(s) for o, s in zip(outs[1:], cast_shapes)]


def _mm_res_kernel(a_ref, w_ref, x_ref, gate_ref, cs_ref, o_ref, *, chunk):
    tm = o_ref.shape[0]
    for c in range(tm // chunk):
        rows = pl.ds(c * chunk, chunk)
        acc = jnp.dot(a_ref[rows, :], w_ref[...], preferred_element_type=F32)
        o_ref[rows, :] = x_ref[rows, :] + gate_ref[...] * (acc * cs_ref[...])


def _mm_res_call(a, w, w_idx, xs, modr, col_scale, layer, gate_chunk, row0, *, tm, tn, grouped=False):
    k = w.shape[-2]
    assert a.shape[0] % tm == 0 and row0 % tm == 0
    n_row_tiles = a.shape[0] // tm
    row_tile0 = row0 // tm
    gate = _mod_row(layer, gate_chunk, tm)
    if grouped:
        assert tn == w.shape[-1]
        grid = (D_MODEL // tn, n_row_tiles)
        at = lambda f: (lambda j, i: f(i, j))
        a_spec = pl.BlockSpec((tm, k), at(lambda i, j: (i, j)))
        w_spec = pl.BlockSpec((None, None, k, tn), at(lambda i, j: (w_idx, j, 0, 0)))
    else:
        grid = (n_row_tiles, D_MODEL // tn)
        at = lambda f: f
        a_spec = pl.BlockSpec((tm, k), lambda i, j: (i, 0))
        w_spec = pl.BlockSpec((None, k, tn), lambda i, j: (w_idx, 0, j))
    x_spec = pl.BlockSpec((tm, tn), at(lambda i, j: (row_tile0 + i, j)))
    vmem = 2 * (tm * k * 2 + k * tn * 2 + 2 * tm * tn * 4) + (12 << 20)
    return pl.pallas_call(
        functools.partial(_mm_res_kernel, chunk=min(tm, 512)),
        grid=grid,
        in_specs=[
            a_spec,
            w_spec,
            x_spec,
            pl.BlockSpec((None, 1, tn), at(lambda i, j: (gate(row_tile0 + i), 0, j))),
            pl.BlockSpec((1, tn), at(lambda i, j: (0, j))),
        ],
        out_specs=x_spec,
        out_shape=jax.ShapeDtypeStruct(xs.shape, F32),
        input_output_aliases={2: 0},
        compiler_params=_params(("arbitrary", "arbitrary"), vmem),
        name="mm_residual",
    )(a, w, xs, modr, col_scale)


def _pool_kernel(x_ref, xp_ref, xn_ref, g_ref, sh_ref, sc_ref, o_ref, hs_ref, *, r, halo, n_lat_tiles):
    i = pl.program_id(0)
    seq_len = jnp.where(i < n_lat_tiles, SEQ, CTX_LEN)
    pos0 = (i * r) & (seq_len - 1)
    is_first = pos0 == 0
    is_last = pos0 + r == seq_len
    g, sh, sc = g_ref[...], sh_ref[...], sc_ref[...]
    hs_ref[0:halo, :] = jnp.where(is_first, 0.0, _norm_mod(xp_ref[...], g, sh, sc))
    hs_ref[halo:halo + r, :] = _norm_mod(x_ref[...], g, sh, sc)
    hs_ref[halo + r:, :] = jnp.where(is_last, 0.0, _norm_mod(xn_ref[...], g, sh, sc))

    rc, cc = 128, 256
    for grp, w in enumerate(POOL_WINDOWS):
        before, after = w // 2, w - 1 - w // 2
        for ri in range(r // rc):
            pos = pos0 + ri * rc + lax.broadcasted_iota(jnp.int32, (rc, 1), 0)
            lo = jnp.maximum(pos - before, 0)
            hi = jnp.minimum(pos + after, seq_len - 1)
            inv_cnt = 1.0 / (hi - lo + 1).astype(F32)
            for ci in range(POOL_GROUP // cc):
                cols = slice(grp * POOL_GROUP + ci * cc, grp * POOL_GROUP + (ci + 1) * cc)
                acc = hs_ref[pl.ds(halo + ri * rc - before, rc), cols]
                for s in range(-before + 1, after + 1):
                    acc = acc + hs_ref[pl.ds(halo + ri * rc + s, rc), cols]
                y = acc * inv_cnt - hs_ref[pl.ds(halo + ri * rc, rc), cols]
                o_ref[pl.ds(ri * rc, rc), cols] = y.astype(o_ref.dtype)


def _pool_call(xs, g, modr, layer, n_rows):
    r, halo = 256, 8
    n_halo_blocks = n_rows // halo
    sh = _mod_row(layer, 0, r)
    sc = _mod_row(layer, 1, r)
    kern = functools.partial(_pool_kernel, r=r, halo=halo, n_lat_tiles=N_LAT // r)
    return pl.pallas_call(
        kern,
        grid=(n_rows // r,),
        in_specs=[
            pl.BlockSpec((r, D_MODEL), lambda i: (i, 0)),
            pl.BlockSpec((halo, D_MODEL), lambda i: (jnp.maximum(i * (r // halo) - 1, 0), 0)),
            pl.BlockSpec((halo, D_MODEL), lambda i: (jnp.minimum((i + 1) * (r // halo), n_halo_blocks - 1), 0)),
            pl.BlockSpec((1, D_MODEL), lambda i: (0, 0)),
            pl.BlockSpec((None, 1, D_MODEL), lambda i: (sh(i), 0, 0)),
            pl.BlockSpec((None, 1, D_MODEL), lambda i: (sc(i), 0, 0)),
        ],
        out_specs=pl.BlockSpec((r, D_MODEL), lambda i: (i, 0)),
        out_shape=jax.ShapeDtypeStruct((n_rows, D_MODEL), BF16),
        scratch_shapes=[pltpu.VMEM((r + 2 * halo, D_MODEL), F32)],
        compiler_params=_params(("arbitrary",), 32 << 20),
        name="norm_mod_pool",
    )(xs, xs, xs, g.reshape(1, D_MODEL), modr, modr)


def _ffn_in_kernel(a_ref, ap_ref, an_ref, wg_ref, wv_ref, cwg_ref, cwv_ref, cbg_ref, cbv_ref, o_ref,
                   aext_ref, u_ref, *, halo, n_lat_tiles, dot_chunks, rc):
    i = pl.program_id(0)
    j = pl.program_id(1)
    tm, half = o_ref.shape

    @pl.when(j == 0)
    def _():
        aext_ref[0:halo, :] = ap_ref[...]
        aext_ref[halo:halo + tm, :] = a_ref[...]
        aext_ref[halo + tm:, :] = an_ref[...]

    row = 0
    for size in dot_chunks:
        rows = pl.ds(row, size)
        a = aext_ref[rows, :]
        u_ref[rows, 0:half] = jnp.dot(a, wg_ref[...], preferred_element_type=F32)
        u_ref[rows, half:] = jnp.dot(a, wv_ref[...], preferred_element_type=F32)
        row += size

    seq_len = jnp.where(i < n_lat_tiles, SEQ, CTX_LEN)
    cw = jnp.concatenate([cwg_ref[...], cwv_ref[...]], axis=1)
    cb = jnp.concatenate([cbg_ref[...], cbv_ref[...]], axis=1)
    sub = lax.broadcasted_iota(jnp.int32, (8, 1), 0)
    for ri in range(tm // rc):
        r0 = ri * rc
        up = u_ref[pl.ds(halo + r0 - 1, rc), :]
        un = u_ref[pl.ds(halo + r0 + 1, rc), :]
        uc = u_ref[pl.ds(halo + r0, rc), :]
        if r0 % CTX_LEN == 0:
            pos = (i * tm + r0 + sub) & (seq_len - 1)
            up = jnp.concatenate([jnp.where(pos != 0, up[0:8], 0.0), up[8:]], axis=0)
        if (r0 + rc) % CTX_LEN == 0:
            pos = (i * tm + r0 + rc - 8 + sub) & (seq_len - 1)
            un = jnp.concatenate([un[:rc - 8], jnp.where(pos != seq_len - 1, un[rc - 8:], 0.0)], axis=0)
        conv = up * cw[0:1] + uc * cw[1:2] + un * cw[2:3] + cb
        gate = conv[:, :half]
        val = conv[:, half:]
        o_ref[pl.ds(r0, rc), :] = (gate * (1.0 / (1.0 + jnp.exp(-gate))) * val).astype(o_ref.dtype)


def _ffn_in_call(h, w_in, w_idx, conv_w, conv_b, layer, n_rows):
    half, halo = 512, BF16_SUBLANES
    n_halo_blocks = n_rows // halo
    ext = TM + 2 * halo
    nj = D_FF // half
    kern = functools.partial(_ffn_in_kernel, halo=halo, n_lat_tiles=N_LAT // TM,
                             dot_chunks=(272, 272, 256, 256), rc=32)
    vmem = (2 * (TM * D_MODEL * 2 + 2 * D_MODEL * half * 2 + TM * half * 2 + 2 * halo * D_MODEL * 2)
            + ext * D_MODEL * 2 + ext * 2 * half * 4 + (12 << 20))
    return pl.pallas_call(
        kern,
        grid=(n_rows // TM, nj),
        in_specs=[
            pl.BlockSpec((TM, D_MODEL), lambda i, j: (i, 0)),
            pl.BlockSpec((halo, D_MODEL), lambda i, j: (jnp.maximum(i * (TM // halo) - 1, 0), 0)),
            pl.BlockSpec((halo, D_MODEL),
                         lambda i, j: (jnp.minimum((i + 1) * (TM // halo), n_halo_blocks - 1), 0)),
            pl.BlockSpec((None, D_MODEL, half), lambda i, j: (w_idx, 0, j)),
            pl.BlockSpec((None, D_MODEL, half), lambda i, j: (w_idx, 0, nj + j)),
            pl.BlockSpec((None, 3, half), lambda i, j: (layer, 0, j)),
            pl.BlockSpec((None, 3, half), lambda i, j: (layer, 0, nj + j)),
            pl.BlockSpec((None, 1, half), lambda i, j: (layer, 0, j)),
            pl.BlockSpec((None, 1, half), lambda i, j: (layer, 0, nj + j)),
        ],
        out_specs=pl.BlockSpec((TM, half), lambda i, j: (i, j)),
        out_shape=jax.ShapeDtypeStruct((n_rows, D_FF), BF16),
        scratch_shapes=[
            pltpu.VMEM((ext, D_MODEL), BF16),
            pltpu.VMEM((ext, 2 * half), F32),
        ],
        compiler_params=_params(("arbitrary", "arbitrary"), vmem),
        name="ffn_in_conv_glu",
    )(h, h, h, w_in, w_in, conv_w, conv_w, conv_b, conv_b)


def _rope_tables():
    rows = SEQ // GRID_W
    row = jnp.repeat(jnp.arange(rows), GRID_W).astype(F32)
    col = jnp.tile(jnp.arange(GRID_W), rows).astype(F32)
    inv = 1.0 / (ROPE_BASE ** (jnp.arange(0, ROPE_HALF, 2, dtype=F32) / ROPE_HALF))

    def tab(pos):
        a = pos[:, None] * inv[None, :]
        a = jnp.concatenate([a, a], axis=-1)
        return jnp.cos(a), jnp.sin(a)

    cr, sr = tab(row)
    cc, sc = tab(col)
    cos = jnp.concatenate([cr, cc], axis=-1)
    sin = jnp.concatenate([sr, sc], axis=-1)
    lane = jnp.arange(HEAD_DIM)
    sin = jnp.where((lane % ROPE_HALF) < ROPE_HALF // 2, -sin, sin)
    cos_all = jnp.concatenate([jnp.tile(cos, (BATCH, 1)), jnp.ones((N_CTX, HEAD_DIM), F32)], axis=0)
    sin_all = jnp.concatenate([jnp.tile(sin, (BATCH, 1)), jnp.zeros((N_CTX, HEAD_DIM), F32)], axis=0)
    return cos_all, sin_all


def _ctx_needed_after(i):
    return any((j % N_MIXERS) == 0 for j in range(i + 1, DEPTH))


def kernel(x, c, ctx, c_ctx, ada_w, ada_b, norm_g, attn_qkv, attn_o, attn_lambda, attn_subln_g,
           pool_w, pool_scale, ffn_w_in, ffn_conv_w, ffn_conv_b, ffn_w_out, final_g):
    assert x.shape == (BATCH, SEQ, D_MODEL) and ctx.shape == (BATCH, CTX_LEN, D_MODEL)
    xs = jnp.concatenate([x.reshape(N_LAT, D_MODEL), ctx.reshape(N_CTX, D_MODEL)], axis=0)
    cond = jnp.concatenate(
        [c, c_ctx[None, :], jnp.zeros((COND_ROWS - BATCH - 1, D_MODEL), F32)], axis=0)
    mod = _ada(cond, ada_w, ada_b)
    modr = mod.reshape(DEPTH, COND_ROWS, 6, D_MODEL).transpose(0, 2, 1, 3).reshape(
        DEPTH * 6 * COND_ROWS, 1, D_MODEL)
    cos_tab, sin_tab = _rope_tables()
    ones_row = jnp.ones((1, D_MODEL), F32)

    conv_b = ffn_conv_b.reshape(DEPTH, 1, 2 * D_FF)
    w16 = {("qkv", 0): attn_qkv[0:1].astype(BF16)}

    for i in range(DEPTH):
        need_ctx = _ctx_needed_after(i)
        n_rows = N_ALL if need_ctx else N_LAT
        if i % N_MIXERS == 0:
            a = i // N_MIXERS
            lam_init = 0.8 - 0.6 * math.exp(-0.3 * i)
            h = _norm_mod_call(xs, norm_g[i, 0], modr, i, 0, N_ALL)
            qkv = _qkv_call(h, w16["qkv", a], 0, cos_tab, sin_tab, N_ALL)
            casts = [("o", a, attn_o), ("in", i, ffn_w_in), ("out", i, ffn_w_out)]
            for nxt in range(i + 1, min(i + N_MIXERS, DEPTH)):
                casts += [("pool", nxt // N_MIXERS, pool_w), ("in", nxt, ffn_w_in), ("out", nxt, ffn_w_out)]
            if i + N_MIXERS < DEPTH:
                casts.append(("qkv", a + 1, attn_qkv))
            o_lat, cast_out = _attn_call(qkv, attn_lambda[a], attn_subln_g[a], lam_init, latent=True,
                                         casts=[(w, idx) for _, idx, w in casts])
            for (name, idx, _), w_bf16 in zip(casts, cast_out):
                w16[name, idx] = w_bf16
            xs = _mm_res_call(o_lat, w16["o", a], 0, xs, modr, ones_row, i, 2, 0, tm=TM, tn=512)
            if need_ctx:
                o_ctx, _ = _attn_call(qkv, attn_lambda[a], attn_subln_g[a], lam_init, latent=False)
                xs = _mm_res_call(o_ctx, w16["o", a], 0, xs, modr, ones_row, i, 2, N_LAT, tm=TM, tn=512)
        else:
            p = i // N_MIXERS
            y = _pool_call(xs, norm_g[i, 0], modr, i, n_rows)
            xs = _mm_res_call(y, w16["pool", p], 0, xs, modr, pool_scale[p].reshape(1, D_MODEL), i, 2, 0,
                              tm=TM, tn=POOL_GROUP, grouped=True)
        h = _norm_mod_call(xs, norm_g[i, 1], modr, i, 3, n_rows)
        act = _ffn_in_call(h, w16["in", i], 0, ffn_conv_w, conv_b, i, n_rows)
        xs = _mm_res_call(act, w16["out", i], 0, xs, modr, ones_row, i, 5, 0, tm=TM, tn=512)
    return _final_norm(xs, final_g).reshape(BATCH, SEQ, D_MODEL)
```
